```python
import math, functools
import jax, jax.numpy as jnp
from jax import lax
import numpy as np

D_MODEL = 1024
BATCH = 8
SEQ = 2048
DEPTH = 2

D_FF = 2816
SSD_HEADS = 16
SSD_HEAD_DIM = 64
SSD_D_INNER = SSD_HEADS * SSD_HEAD_DIM
SSD_GROUPS = 2
SSD_STATE = 128
SSD_CONV = 4
SSD_CHUNK = 128
SSD_CONV_DIM = SSD_D_INNER + 2 * SSD_GROUPS * SSD_STATE
MLA_HEADS = 8
MLA_Q_RANK = 384
MLA_KV_RANK = 256
MLA_NOPE = 128
MLA_ROPE = 64
MLA_V = 128
ROPE_THETA = 10000.0
ATTN_BLOCK = 128
SGU_WIDTH = 2 * D_MODEL
SGU_GROUPS = 16
SGU_GROUP_DIM = SGU_WIDTH // SGU_GROUPS
SGU_CHUNK = 128
EVEN_SPLITS = [SSD_D_INNER,
               SSD_D_INNER + SSD_CONV_DIM,
               SSD_D_INNER + SSD_CONV_DIM + SSD_HEADS,
               SSD_D_INNER + SSD_CONV_DIM + SSD_HEADS + MLA_Q_RANK,
               SSD_D_INNER + SSD_CONV_DIM + SSD_HEADS + MLA_Q_RANK + MLA_KV_RANK]
EVEN_IN = EVEN_SPLITS[-1] + MLA_ROPE
EVEN_MIX = SSD_D_INNER + MLA_HEADS * MLA_V
DEEPNORM_ALPHA = (2 * DEPTH) ** 0.25
DEEPNORM_BETA = (8 * DEPTH) ** -0.25
N_MOD = 9
EPS = 1e-5

kernel_name = "conditioned_hybrid_ssd_mla_sgu_trunk"


def layer_norm(x, g, b):
    xf = x.astype(jnp.float32)
    mu = jnp.mean(xf, -1, keepdims=True)
    var = jnp.mean(jnp.square(xf - mu), -1, keepdims=True)
    return ((xf - mu) * lax.rsqrt(var + EPS) * g + b).astype(x.dtype)


def rms_norm(x, w):
    xf = x.astype(jnp.float32)
    return (xf * lax.rsqrt(jnp.mean(xf * xf, -1, keepdims=True) + EPS) * w).astype(x.dtype)


def swiglu(h, w_in, w_out):
    a, b = jnp.split(h @ w_in, 2, axis=-1)
    return (jax.nn.silu(a) * b) @ w_out


def rope_tables(seq):
    inv = 1.0 / (ROPE_THETA ** (jnp.arange(0, MLA_ROPE, 2, dtype=jnp.float32) / MLA_ROPE))
    ang = jnp.arange(seq, dtype=jnp.float32)[:, None] * inv[None, :]
    return jnp.cos(ang), jnp.sin(ang)


def apply_rope(x, cos, sin):
    x1, x2 = jnp.split(x.astype(jnp.float32), 2, axis=-1)
    return jnp.concatenate([x1 * cos - x2 * sin, x2 * cos + x1 * sin], -1).astype(x.dtype)


def causal_depthwise_conv(x, w, bias):
    out = lax.conv_general_dilated(x, w[:, None, :], window_strides=(1,),
                                   padding=[(SSD_CONV - 1, 0)],
                                   dimension_numbers=('NWC', 'WIO', 'NWC'),
                                   feature_group_count=x.shape[-1])
    return out + bias


def segsum_exp(a):
    q = a.shape[-1]
    cs = jnp.cumsum(a, axis=-1)
    diff = cs[..., :, None] - cs[..., None, :]
    mask = jnp.tril(jnp.ones((q, q), dtype=bool))
    return jnp.where(mask, jnp.exp(jnp.where(mask, diff, 0.0)), 0.0)


def ssd_chunked_scan(x, dt, a, bm, cm):
    b, s = x.shape[:2]
    nc = s // SSD_CHUNK
    r = SSD_HEADS // SSD_GROUPS
    xc = (x * dt[..., None]).reshape(b, nc, SSD_CHUNK, SSD_GROUPS, r, SSD_HEAD_DIM)
    adt = (dt * a).reshape(b, nc, SSD_CHUNK, SSD_GROUPS, r).transpose(0, 3, 4, 1, 2)
    bc = bm.reshape(b, nc, SSD_CHUNK, SSD_GROUPS, SSD_STATE)
    cc = cm.reshape(b, nc, SSD_CHUNK, SSD_GROUPS, SSD_STATE)
    a_cs = jnp.cumsum(adt, axis=-1)
    decay = segsum_exp(adt)
    cb = jnp.einsum('bclgn,bcsgn->bgcls', cc, bc)
    y_diag = jnp.einsum('bgrcls,bcsgrp->bclgrp', cb[:, :, None] * decay, xc)
    decay_to_end = jnp.exp(a_cs[..., -1:] - a_cs)
    states = jnp.einsum('bclgn,bgrcl,bclgrp->bcgrpn', bc, decay_to_end, xc)
    chunk_decay = jnp.exp(a_cs[..., -1])

    def step(h, inp):
        s_c, d_c = inp
        return d_c[..., None, None] * h + s_c, h

    h0 = jnp.zeros((b, SSD_GROUPS, r, SSD_HEAD_DIM, SSD_STATE), states.dtype)
    _, prev = lax.scan(step, h0, (states.transpose(1, 0, 2, 3, 4, 5),
                                  chunk_decay.transpose(3, 0, 1, 2)))
    prev = prev.transpose(1, 0, 2, 3, 4, 5)
    y_off = jnp.einsum('bclgn,bcgrpn,bgrcl->bclgrp', cc, prev, jnp.exp(a_cs))
    return (y_diag + y_off).reshape(b, s, SSD_HEADS, SSD_HEAD_DIM)


def mla_attention(cq, ckv, k_rope, q_norm_w, w_uq, kv_norm_w, w_ukv, cos, sin):
    b, s, _ = cq.shape
    q = (rms_norm(cq, q_norm_w) @ w_uq).reshape(b, s, MLA_HEADS, MLA_NOPE + MLA_ROPE)
    q_nope, q_pe = q[..., :MLA_NOPE], q[..., MLA_NOPE:]
    q_pe = apply_rope(q_pe, cos[:, None, :], sin[:, None, :])
    kv = (rms_norm(ckv, kv_norm_w) @ w_ukv).reshape(b, s, MLA_HEADS, MLA_NOPE + MLA_V)
    k_nope, v = kv[..., :MLA_NOPE], kv[..., MLA_NOPE:]
    k_pe = apply_rope(k_rope, cos, sin)
    k = jnp.concatenate([k_nope, jnp.broadcast_to(k_pe[:, :, None, :], (b, s, MLA_HEADS, MLA_ROPE))], -1)
    qh = jnp.concatenate([q_nope, q_pe], -1)
    scale = (MLA_NOPE + MLA_ROPE) ** -0.5
    outs = []
    for i in range(s // ATTN_BLOCK):
        start, end = i * ATTN_BLOCK, (i + 1) * ATTN_BLOCK
        sc = jnp.einsum('bqhd,bkhd->bhqk', qh[:, start:end], k[:, :end]).astype(jnp.float32) * scale
        mask = (start + jnp.arange(ATTN_BLOCK))[:, None] >= jnp.arange(end)[None, :]
        p = jax.nn.softmax(jnp.where(mask, sc, -jnp.inf), axis=-1).astype(v.dtype)
        outs.append(jnp.einsum('bhqk,bkhd->bqhd', p, v[:, :end]))
    return jnp.concatenate(outs, axis=1).reshape(b, s, MLA_HEADS * MLA_V)


def even_mixer(h, w_in, conv_w, conv_b, dt_bias, a_log, d_skip, ssd_norm_w,
               q_norm_w, w_uq, kv_norm_w, w_ukv, w_out, cos, sin):
    b, s, _ = h.shape
    z, xbc, dt_raw, cq, ckv, k_rope = jnp.split(h @ w_in, EVEN_SPLITS, axis=-1)
    xbc = jax.nn.silu(causal_depthwise_conv(xbc, conv_w, conv_b))
    xs, bm, cm = jnp.split(xbc, [SSD_D_INNER, SSD_D_INNER + SSD_GROUPS * SSD_STATE], axis=-1)
    xs4 = xs.reshape(b, s, SSD_HEADS, SSD_HEAD_DIM).astype(jnp.float32)
    dt = jax.nn.softplus(dt_raw.astype(jnp.float32) + dt_bias)
    a = -jnp.exp(a_log.astype(jnp.float32))
    y = ssd_chunked_scan(xs4, dt, a,
                         bm.reshape(b, s, SSD_GROUPS, SSD_STATE).astype(jnp.float32),
                         cm.reshape(b, s, SSD_GROUPS, SSD_STATE).astype(jnp.float32))
    y = (y + d_skip[:, None] * xs4).reshape(b, s, SSD_D_INNER).astype(h.dtype)
    yg = (y * jax.nn.silu(z)).reshape(b, s, SSD_GROUPS, SSD_D_INNER // SSD_GROUPS)
    y_ssd = rms_norm(yg, ssd_norm_w.reshape(SSD_GROUPS, -1)).reshape(b, s, SSD_D_INNER)
    y_att = mla_attention(cq, ckv, k_rope, q_norm_w, w_uq, kv_norm_w, w_ukv, cos, sin)
    return jnp.concatenate([y_ssd, y_att], axis=-1) @ w_out


def odd_mixer(h, w_uv, b_uv, sgu_ln_g, sgu_ln_b, w_s, b_s, w_out):
    b, s, _ = h.shape
    u, v = jnp.split(jax.nn.gelu(h @ w_uv + b_uv, approximate=False), 2, axis=-1)
    v = layer_norm(v, sgu_ln_g, sgu_ln_b)
    vc = v.reshape(b, s // SGU_CHUNK, SGU_CHUNK, SGU_GROUPS, SGU_GROUP_DIM)
    w_causal = w_s * jnp.tril(jnp.ones((SGU_CHUNK, SGU_CHUNK), w_s.dtype))
    sp = jnp.einsum('gts,bcsgd->bctgd', w_causal, vc) + b_s.T[None, None, :, :, None]
    return (u * sp.reshape(b, s, SGU_WIDTH)) @ w_out


def hybrid_layer(x, c, ada_w, ada_b, ln_g, ln_b, ffa_w_in, ffa_w_out, ffb_w_in, ffb_w_out, mixer):
    mod = jax.nn.silu(c) @ ada_w + ada_b
    sh1, sc1, g1, sh2, sc2, g2, sh3, sc3, g3 = [m[:, None, :] for m in jnp.split(mod, N_MOD, axis=-1)]
    f1 = 0.5 * swiglu(x * (1.0 + sc1) + sh1, ffa_w_in, ffa_w_out)
    x = layer_norm(DEEPNORM_ALPHA * x + (1.0 + g1) * f1, ln_g[0], ln_b[0])
    m = mixer(x * (1.0 + sc2) + sh2)
    x = layer_norm(DEEPNORM_ALPHA * x + (1.0 + g2) * m, ln_g[1], ln_b[1])
    f2 = 0.5 * swiglu(x * (1.0 + sc3) + sh3, ffb_w_in, ffb_w_out)
    return layer_norm(DEEPNORM_ALPHA * x + (1.0 + g3) * f2, ln_g[2], ln_b[2])


def setup_inputs(seed: int = 0) -> dict:
    key = jax.random.key(seed)
    keys = iter(jax.random.split(key, 64))
    f32 = jnp.float32

    def nrm(shape, scale):
        return jax.random.normal(next(keys), shape, f32) * scale

    def gain(shape):
        return 1.0 + nrm(shape, 0.02)

    def common(prefix):
        return {
            prefix + 'ada_w': nrm((D_MODEL, N_MOD * D_MODEL), 0.2 * D_MODEL ** -0.5),
            prefix + 'ada_b': nrm((N_MOD * D_MODEL,), 0.01),
            prefix + 'ln_g': gain((3, D_MODEL)),
            prefix + 'ln_b': nrm((3, D_MODEL), 0.01),
            prefix + 'ffa_w_in': nrm((D_MODEL, 2 * D_FF), D_MODEL ** -0.5),
            prefix + 'ffa_w_out': nrm((D_FF, D_MODEL), D_FF ** -0.5 * DEEPNORM_BETA),
            prefix + 'ffb_w_in': nrm((D_MODEL, 2 * D_FF), D_MODEL ** -0.5),
            prefix + 'ffb_w_out': nrm((D_FF, D_MODEL), D_FF ** -0.5 * DEEPNORM_BETA),
        }

    out = {'x': nrm((BATCH, SEQ, D_MODEL), 1.0), 'c': nrm((BATCH, D_MODEL), 1.0)}
    out.update(common('l0_'))
    dt0 = jnp.exp(jax.random.uniform(next(keys), (SSD_HEADS,), f32)
                  * (math.log(0.1) - math.log(0.001)) + math.log(0.001))
    out.update({
        'l0_w_in': nrm((D_MODEL, EVEN_IN), D_MODEL ** -0.5),
        'l0_conv_w': nrm((SSD_CONV, SSD_CONV_DIM), SSD_CONV ** -0.5),
        'l0_conv_b': nrm((SSD_CONV_DIM,), 0.01),
        'l0_dt_bias': dt0 + jnp.log(-jnp.expm1(-dt0)),
        'l0_a_log': jnp.log(jax.random.uniform(next(keys), (SSD_HEADS,), f32, 1.0, 16.0)),
        'l0_d_skip': 1.0 + nrm((SSD_HEADS,), 0.1),
        'l0_ssd_norm_w': gain((SSD_D_INNER,)),
        'l0_q_norm_w': gain((MLA_Q_RANK,)),
        'l0_w_uq': nrm((MLA_Q_RANK, MLA_HEADS * (MLA_NOPE + MLA_ROPE)), MLA_Q_RANK ** -0.5),
        'l0_kv_norm_w': gain((MLA_KV_RANK,)),
        'l0_w_ukv': nrm((MLA_KV_RANK, MLA_HEADS * (MLA_NOPE + MLA_V)), MLA_KV_RANK ** -0.5),
        'l0_w_out': nrm((EVEN_MIX, D_MODEL), EVEN_MIX ** -0.5 * DEEPNORM_BETA),
    })
    out.update(common('l1_'))
    out.update({
        'l1_w_uv': nrm((D_MODEL, 2 * SGU_WIDTH), D_MODEL ** -0.5),
        'l1_b_uv': nrm((2 * SGU_WIDTH,), 0.01),
        'l1_sgu_ln_g': gain((SGU_WIDTH,)),
        'l1_sgu_ln_b': nrm((SGU_WIDTH,), 0.01),
        'l1_w_s': nrm((SGU_GROUPS, SGU_CHUNK, SGU_CHUNK), SGU_CHUNK ** -0.5),
        'l1_b_s': 1.0 + nrm((SGU_GROUPS, SGU_CHUNK), 0.1),
        'l1_w_out': nrm((SGU_WIDTH, D_MODEL), SGU_WIDTH ** -0.5 * DEEPNORM_BETA),
    })
    return out


def reference(x, c, l0_ada_w, l0_ada_b, l0_ln_g, l0_ln_b, l0_ffa_w_in, l0_ffa_w_out,
              l0_ffb_w_in, l0_ffb_w_out, l0_w_in, l0_conv_w, l0_conv_b, l0_dt_bias,
              l0_a_log, l0_d_skip, l0_ssd_norm_w, l0_q_norm_w, l0_w_uq, l0_kv_norm_w,
              l0_w_ukv, l0_w_out, l1_ada_w, l1_ada_b, l1_ln_g, l1_ln_b, l1_ffa_w_in,
              l1_ffa_w_out, l1_ffb_w_in, l1_ffb_w_out, l1_w_uv, l1_b_uv, l1_sgu_ln_g,
              l1_sgu_ln_b, l1_w_s, l1_b_s, l1_w_out):
    cos, sin = rope_tables(x.shape[1])
    commons = [
        (l0_ada_w, l0_ada_b, l0_ln_g, l0_ln_b, l0_ffa_w_in, l0_ffa_w_out, l0_ffb_w_in, l0_ffb_w_out),
        (l1_ada_w, l1_ada_b, l1_ln_g, l1_ln_b, l1_ffa_w_in, l1_ffa_w_out, l1_ffb_w_in, l1_ffb_w_out),
    ]
    mixers = [
        functools.partial(even_mixer, w_in=l0_w_in, conv_w=l0_conv_w, conv_b=l0_conv_b,
                          dt_bias=l0_dt_bias, a_log=l0_a_log, d_skip=l0_d_skip,
                          ssd_norm_w=l0_ssd_norm_w, q_norm_w=l0_q_norm_w, w_uq=l0_w_uq,
                          kv_norm_w=l0_kv_norm_w, w_ukv=l0_w_ukv, w_out=l0_w_out,
                          cos=cos, sin=sin),
        functools.partial(odd_mixer, w_uv=l1_w_uv, b_uv=l1_b_uv, sgu_ln_g=l1_sgu_ln_g,
                          sgu_ln_b=l1_sgu_ln_b, w_s=l1_w_s, b_s=l1_b_s, w_out=l1_w_out),
    ]
    for layer in range(DEPTH):
        x = hybrid_layer(x, c, *commons[layer], mixers[layer])
    return x
```

```python
import functools
import math

import jax
import jax.numpy as jnp
from jax import lax
from jax.experimental import pallas as pl
from jax.experimental.pallas import tpu as pltpu

F32 = jnp.float32
BF16 = jnp.bfloat16

D_MODEL = 1024
DEPTH = 2
D_FF = 2816
SSD_HEADS = 16
SSD_HEAD_DIM = 64
SSD_D_INNER = SSD_HEADS * SSD_HEAD_DIM
SSD_GROUPS = 2
SSD_STATE = 128
SSD_CONV = 4
SSD_CHUNK = 128
SSD_CONV_DIM = SSD_D_INNER + 2 * SSD_GROUPS * SSD_STATE
MLA_HEADS = 8
MLA_Q_RANK = 384
MLA_KV_RANK = 256
MLA_NOPE = 128
MLA_ROPE = 64
MLA_V = 128
ROPE_THETA = 10000.0
SGU_WIDTH = 2 * D_MODEL
SGU_GROUPS = 16
SGU_GROUP_DIM = SGU_WIDTH // SGU_GROUPS
SGU_CHUNK = 128
DEEPNORM_ALPHA = (2 * DEPTH) ** 0.25
N_MOD = 9
EPS = 1e-5

LANES = 128
SUBLANES = 8
VMEM_BYTES_V7X = 64 * 1024 * 1024
VMEM_LIMIT = (VMEM_BYTES_V7X * 3) // 4

QK_SLAB = MLA_NOPE + LANES

_OFF_Z = 0
_OFF_XBC = _OFF_Z + SSD_D_INNER
_OFF_DT = _OFF_XBC + SSD_CONV_DIM
_OFF_CQ = _OFF_DT + LANES
_OFF_CKV = _OFF_CQ + MLA_Q_RANK
_OFF_KPE = _OFF_CKV + MLA_KV_RANK
_OFF_KPE_ROT = _OFF_KPE + LANES
_PROJ_COLS = _OFF_KPE_ROT + LANES


def _silu(x):
    return x * jax.nn.sigmoid(x)


def _layer_norm(y, g, b):
    mu = jnp.mean(y, axis=-1, keepdims=True)
    yc = y - mu
    var = jnp.mean(yc * yc, axis=-1, keepdims=True)
    return yc * lax.rsqrt(var + EPS) * g + b


def _rms_norm(y, w):
    return y * lax.rsqrt(jnp.mean(y * y, axis=-1, keepdims=True) + EPS) * w


def _dot(a, b):
    return jnp.dot(a, b, preferred_element_type=F32)


def _dot_nt(a, b):
    return lax.dot_general(a, b, (((1,), (1,)), ((), ())), preferred_element_type=F32)


def _dot_tn(a, b):
    return lax.dot_general(a, b, (((0,), (0,)), ((), ())), preferred_element_type=F32)


def _split_bf16(x, terms):
    parts = []
    r = x
    for _ in range(terms):
        p = r.astype(BF16)
        parts.append(p)
        r = r - p.astype(F32)
    return parts


def _dot_split(a_bf16, x, terms):
    acc = None
    for p in _split_bf16(x, terms):
        t = _dot(a_bf16, p)
        acc = t if acc is None else acc + t
    return acc


def _dot_split_rhs(x, b_bf16, terms):
    acc = None
    for p in _split_bf16(x, terms):
        t = _dot(p, b_bf16)
        acc = t if acc is None else acc + t
    return acc


def _resident(shape):
    n = len(shape)
    return pl.BlockSpec(shape, lambda *_: (0,) * n, pipeline_mode=pl.Buffered(1))


def _params(n_axes):
    return pltpu.CompilerParams(dimension_semantics=("arbitrary",) * n_axes,
                                vmem_limit_bytes=VMEM_LIMIT)


def _ada_kernel(c_ref, w_ref, b_ref, o_ref):
    s = _silu(c_ref[...]).astype(BF16)
    o_ref[...] = _dot(s, w_ref[...].astype(BF16)) + b_ref[...]


def _ada(c, ada_w, ada_b):
    b, d = c.shape
    n = ada_w.shape[1]
    tn = D_MODEL
    return pl.pallas_call(
        _ada_kernel,
        grid=(n // tn,),
        in_specs=[pl.BlockSpec((b, d), lambda j: (0, 0)),
                  pl.BlockSpec((d, tn), lambda j: (0, j)),
                  pl.BlockSpec((1, tn), lambda j: (0, j))],
        out_specs=pl.BlockSpec((b, tn), lambda j: (0, j)),
        out_shape=jax.ShapeDtypeStruct((b, n), F32),
        compiler_params=_params(1),
        name="ada_mod",
    )(c, ada_w, ada_b.reshape(1, n))


def _mod_rows(mod_ref, k):
    return (mod_ref[0, 3 * k:3 * k + 1, :], mod_ref[0, 3 * k + 1:3 * k + 2, :],
            mod_ref[0, 3 * k + 2:3 * k + 3, :])


def _ffn_kernel(x_ref, mod_ref, win_ref, wout_ref, g_ref, b_ref, o_ref, *, sub):
    x = x_ref[...]
    sh, sc, gate = _mod_rows(mod_ref, sub)
    h = (x * (1.0 + sc) + sh).astype(BF16)
    ab = _dot(h, win_ref[...])
    a = ab[:, :D_FF]
    b = ab[:, D_FF:]
    act = (_silu(a) * b).astype(BF16)
    f = _dot(act, wout_ref[...])
    y = DEEPNORM_ALPHA * x + (1.0 + gate) * (0.5 * f)
    o_ref[...] = _layer_norm(y, g_ref[sub:sub + 1, :], b_ref[sub:sub + 1, :])


def _ffn(x2, mod3, w_in, w_out, ln_g, ln_b, *, sub, seq, tm):
    t = x2.shape[0]
    tiles_per_seq = seq // tm
    return pl.pallas_call(
        functools.partial(_ffn_kernel, sub=sub),
        grid=(t // tm,),
        in_specs=[pl.BlockSpec((tm, D_MODEL), lambda i: (i, 0)),
                  pl.BlockSpec((1, N_MOD, D_MODEL), lambda i: (i // tiles_per_seq, 0, 0)),
                  _resident(w_in.shape), _resident(w_out.shape),
                  _resident(ln_g.shape), _resident(ln_b.shape)],
        out_specs=pl.BlockSpec((tm, D_MODEL), lambda i: (i, 0)),
        out_shape=jax.ShapeDtypeStruct((t, D_MODEL), F32),
        compiler_params=_params(1),
        name="ffn_sublayer",
    )(x2, mod3, w_in, w_out, ln_g, ln_b)


def _even_proj_kernel(x_ref, mod_ref, wc_ref, qnw_ref, wq_ref, kvnw_ref, wkv_ref, cos_ref, sin_ref,
                      z_ref, xbc_ref, dt_ref, q_ref, k_ref, v_ref):
    x = x_ref[...]
    sh, sc, _ = _mod_rows(mod_ref, 1)
    h = (x * (1.0 + sc) + sh).astype(BF16)
    p = _dot(h, wc_ref[...])
    z_ref[...] = p[:, _OFF_Z:_OFF_XBC]
    xbc_ref[...] = p[:, _OFF_XBC:_OFF_DT]
    dt_ref[...] = p[:, _OFF_DT:_OFF_CQ]
    cos = cos_ref[...]
    sin = sin_ref[...]
    k_pe = (p[:, _OFF_KPE:_OFF_KPE_ROT] * cos + p[:, _OFF_KPE_ROT:_PROJ_COLS] * sin).astype(BF16)

    cqn = _rms_norm(p[:, _OFF_CQ:_OFF_CKV], qnw_ref[...]).astype(BF16)
    q = _dot(cqn, wq_ref[...])
    scale = (MLA_NOPE + MLA_ROPE) ** -0.5
    hw = MLA_HEADS * LANES
    for hh in range(MLA_HEADS):
        lo = hh * LANES
        q_nope = q[:, lo:lo + LANES]
        q_pe = q[:, hw + lo:hw + lo + LANES] * cos + q[:, 2 * hw + lo:2 * hw + lo + LANES] * sin
        q_ref[:, hh * QK_SLAB:hh * QK_SLAB + MLA_NOPE] = (q_nope * scale).astype(BF16)
        q_ref[:, hh * QK_SLAB + MLA_NOPE:(hh + 1) * QK_SLAB] = (q_pe * scale).astype(BF16)

    ckvn = _rms_norm(p[:, _OFF_CKV:_OFF_KPE], kvnw_ref[...]).astype(BF16)
    kv = _dot(ckvn, wkv_ref[...])
    for hh in range(MLA_HEADS):
        k_ref[:, hh * QK_SLAB:hh * QK_SLAB + MLA_NOPE] = kv[:, hh * LANES:(hh + 1) * LANES].astype(BF16)
        k_ref[:, hh * QK_SLAB + MLA_NOPE:(hh + 1) * QK_SLAB] = k_pe
    v_ref[...] = kv[:, hw:].astype(BF16)


def _even_proj(x2, mod3, wc, qnw, wq, kvnw, wkv, cos_t, sin_t, *, seq, tm):
    t = x2.shape[0]
    tiles_per_seq = seq // tm
    row = lambda i: (i, 0)
    pos = lambda i: (i % tiles_per_seq, 0)
    out_shapes = (jax.ShapeDtypeStruct((t, SSD_D_INNER), F32),
                  jax.ShapeDtypeStruct((t, SSD_CONV_DIM), F32),
                  jax.ShapeDtypeStruct((t, LANES), F32),
                  jax.ShapeDtypeStruct((t, MLA_HEADS * QK_SLAB), BF16),
                  jax.ShapeDtypeStruct((t, MLA_HEADS * QK_SLAB), BF16),
                  jax.ShapeDtypeStruct((t, MLA_HEADS * MLA_V), BF16))
    return pl.pallas_call(
        _even_proj_kernel,
        grid=(t // tm,),
        in_specs=[pl.BlockSpec((tm, D_MODEL), row),
                  pl.BlockSpec((1, N_MOD, D_MODEL), lambda i: (i // tiles_per_seq, 0, 0)),
                  _resident(wc.shape), _resident(qnw.shape), _resident(wq.shape),
                  _resident(kvnw.shape), _resident(wkv.shape),
                  pl.BlockSpec((tm, LANES), pos), pl.BlockSpec((tm, LANES), pos)],
        out_specs=tuple(pl.BlockSpec((tm, s.shape[1]), row) for s in out_shapes),
        out_shape=out_shapes,
        compiler_params=_params(1),
        name="even_proj",
    )(x2, mod3, wc, qnw, wq, kvnw, wkv, cos_t, sin_t)


def _ssd_kernel(xbc_ref, dt_ref, z_ref, convw_ref, convb_ref, dtb_ref, alog_ref, dskip_ref, nw_ref,
                o_ref, state_ref, xwin_ref, y_ref):
    q = SSD_CHUNK
    c = pl.program_id(1)

    @pl.when(c == 0)
    def _():
        state_ref[...] = jnp.zeros_like(state_ref)
        xwin_ref[0:SUBLANES, :] = jnp.zeros((SUBLANES, SSD_CONV_DIM), F32)

    xwin_ref[SUBLANES:SUBLANES + q, :] = xbc_ref[...]
    conv = convb_ref[...]
    for k in range(SSD_CONV):
        lo = SUBLANES - (SSD_CONV - 1) + k
        conv = conv + convw_ref[k:k + 1, :] * xwin_ref[lo:lo + q, :]
    xwin_ref[0:SUBLANES, :] = xbc_ref[q - SUBLANES:q, :]
    xbc = _silu(conv)
    xs = xbc[:, :SSD_D_INNER]
    bm = xbc[:, SSD_D_INNER:SSD_D_INNER + SSD_GROUPS * SSD_STATE]
    cm = xbc[:, SSD_D_INNER + SSD_GROUPS * SSD_STATE:]

    lane = lax.broadcasted_iota(jnp.int32, (1, LANES), 1)
    head_lane = lane < SSD_HEADS
    dt = jax.nn.softplus(dt_ref[...] + dtb_ref[...])
    a = jnp.where(head_lane, -jnp.exp(alog_ref[...]), 0.0)
    adt = dt * a
    row = lax.broadcasted_iota(jnp.int32, (q, q), 0)
    col = lax.broadcasted_iota(jnp.int32, (q, q), 1)
    tril = row >= col
    tril_b = jnp.where(tril, 1.0, 0.0).astype(BF16)
    cs = _dot_split(tril_b, adt, 3)
    cs_t = cs.T
    cs_last = cs[q - 1:q, :]
    ecs = jnp.exp(cs)
    dte = jnp.exp(cs_last - cs)

    er = lax.broadcasted_iota(jnp.int32, (LANES, SSD_D_INNER), 0)
    ec = lax.broadcasted_iota(jnp.int32, (LANES, SSD_D_INNER), 1)
    expand = jnp.where(ec // SSD_HEAD_DIM == er, 1.0, 0.0).astype(BF16)
    dt_x = _dot_split_rhs(dt, expand, 2)
    ecs_x = _dot_split_rhs(ecs, expand, 2)
    dte_x = _dot_split_rhs(dte, expand, 2)

    xc = xs * dt_x
    xc_b = xc.astype(BF16)
    xd_b = (xc * dte_x).astype(BF16)
    bm_b = bm.astype(BF16)
    cm_b = cm.astype(BF16)
    heads_per_group = SSD_HEADS // SSD_GROUPS
    for g in range(SSD_GROUPS):
        bg = bm_b[:, g * SSD_STATE:(g + 1) * SSD_STATE]
        cg = cm_b[:, g * SSD_STATE:(g + 1) * SSD_STATE]
        cb = _dot_nt(cg, bg)
        for r in range(heads_per_group):
            hd = g * heads_per_group + r
            hs = slice(hd * SSD_HEAD_DIM, (hd + 1) * SSD_HEAD_DIM)
            diff = cs[:, hd:hd + 1] - cs_t[hd:hd + 1, :]
            decay = jnp.where(tril, jnp.exp(jnp.where(tril, diff, 0.0)), 0.0)
            w = (cb * decay).astype(BF16)
            st = state_ref[hd]
            y = _dot(w, xc_b[:, hs]) + ecs_x[:, hs] * _dot(cg, st.astype(BF16))
            y_ref[:, hs] = y
            state_ref[hd] = ecs_x[q - 1:q, hs] * st + _dot_tn(bg, xd_b[:, hs])

    y = y_ref[...] + dskip_ref[...] * xs
    y = y * _silu(z_ref[...])
    gw = SSD_D_INNER // SSD_GROUPS
    for g in range(SSD_GROUPS):
        sl = slice(g * gw, (g + 1) * gw)
        o_ref[:, sl] = _rms_norm(y[:, sl], nw_ref[:, sl]).astype(BF16)


def _ssd(xbc, dt, z, conv_w, conv_b, dt_bias, a_log, d_skip_x, norm_w, *, batch, seq):
    q = SSD_CHUNK
    nc = seq // q
    t = batch * seq
    row = lambda b, c: (b * nc + c, 0)
    return pl.pallas_call(
        _ssd_kernel,
        grid=(batch, nc),
        in_specs=[pl.BlockSpec((q, SSD_CONV_DIM), row),
                  pl.BlockSpec((q, LANES), row),
                  pl.BlockSpec((q, SSD_D_INNER), row),
                  _resident(conv_w.shape), _resident(conv_b.shape), _resident(dt_bias.shape),
                  _resident(a_log.shape), _resident(d_skip_x.shape), _resident(norm_w.shape)],
        out_specs=pl.BlockSpec((q, SSD_D_INNER), row),
        out_shape=jax.ShapeDtypeStruct((t, SSD_D_INNER), BF16),
        scratch_shapes=[pltpu.VMEM((SSD_HEADS, SSD_STATE, SSD_HEAD_DIM), F32),
                        pltpu.VMEM((SUBLANES + q, SSD_CONV_DIM), F32),
                        pltpu.VMEM((q, SSD_D_INNER), F32)],
        compiler_params=_params(2),
        name="ssd_scan",
    )(xbc, dt, z, conv_w, conv_b, dt_bias, a_log, d_skip_x, norm_w)


def _attn_kernel(q_ref, k_ref, v_ref, o_ref, *, tq):
    seq = q_ref.shape[0]
    row = lax.broadcasted_iota(jnp.int32, (tq, tq), 0)
    col = lax.broadcasted_iota(jnp.int32, (tq, tq), 1)
    causal = row >= col
    for i in range(seq // tq):
        off = i * tq
        qt = q_ref[off:off + tq, :]
        s_d = jnp.where(causal, _dot_nt(qt, k_ref[off:off + tq, :]), -jnp.inf)
        m = jnp.max(s_d, axis=-1, keepdims=True)
        if off:
            s_o = _dot_nt(qt, k_ref[0:off, :])
            m = jnp.maximum(m, jnp.max(s_o, axis=-1, keepdims=True))
        p_d = jnp.exp(s_d - m)
        l = jnp.sum(p_d, axis=-1, keepdims=True)
        acc = _dot(p_d.astype(BF16), v_ref[off:off + tq, :])
        if off:
            p_o = jnp.exp(s_o - m)
            l = l + jnp.sum(p_o, axis=-1, keepdims=True)
            acc = acc + _dot(p_o.astype(BF16), v_ref[0:off, :])
        o_ref[off:off + tq, :] = (acc / l).astype(BF16)


def _attention(q, k, v, *, batch, seq, tq):
    t = batch * seq
    return pl.pallas_call(
        functools.partial(_attn_kernel, tq=tq),
        grid=(batch, MLA_HEADS),
        in_specs=[pl.BlockSpec((seq, QK_SLAB), lambda b, h: (b, h)),
                  pl.BlockSpec((seq, QK_SLAB), lambda b, h: (b, h)),
                  pl.BlockSpec((seq, MLA_V), lambda b, h: (b, h))],
        out_specs=pl.BlockSpec((seq, MLA_V), lambda b, h: (b, h)),
        out_shape=jax.ShapeDtypeStruct((t, MLA_HEADS * MLA_V), BF16),
        compiler_params=_params(2),
        name="mla_attention",
    )(q, k, v)


def _even_out_kernel(x_ref, mod_ref, ys_ref, ya_ref, wo_ref, g_ref, b_ref, o_ref):
    x = x_ref[...]
    _, _, gate = _mod_rows(mod_ref, 1)
    m = _dot(ys_ref[...], wo_ref[0:SSD_D_INNER, :]) + _dot(ya_ref[...], wo_ref[SSD_D_INNER:, :])
    y = DEEPNORM_ALPHA * x + (1.0 + gate) * m
    o_ref[...] = _layer_norm(y, g_ref[1:2, :], b_ref[1:2, :])


def _even_out(x2, mod3, y_ssd, y_att, w_out, ln_g, ln_b, *, seq, tm):
    t = x2.shape[0]
    tiles_per_seq = seq // tm
    row = lambda i: (i, 0)
    return pl.pallas_call(
        _even_out_kernel,
        grid=(t // tm,),
        in_specs=[pl.BlockSpec((tm, D_MODEL), row),
                  pl.BlockSpec((1, N_MOD, D_MODEL), lambda i: (i // tiles_per_seq, 0, 0)),
                  pl.BlockSpec((tm, SSD_D_INNER), row),
                  pl.BlockSpec((tm, MLA_HEADS * MLA_V), row),
                  _resident(w_out.shape), _resident(ln_g.shape), _resident(ln_b.shape)],
        out_specs=pl.BlockSpec((tm, D_MODEL), row),
        out_shape=jax.ShapeDtypeStruct((t, D_MODEL), F32),
        compiler_params=_params(1),
        name="even_out",
    )(x2, mod3, y_ssd, y_att, w_out, ln_g, ln_b)


def _sgu_kernel(x_ref, mod_ref, wuv_ref, buv_ref, sg_ref, sb_ref, ws_ref, bs_ref, wo_ref, g_ref, b_ref,
                o_ref, gated_ref, *, tm):
    x = x_ref[...]
    sh, sc, gate = _mod_rows(mod_ref, 1)
    h = (x * (1.0 + sc) + sh).astype(BF16)
    uv = _dot(h, wuv_ref[...]) + buv_ref[...]
    uv = 0.5 * uv * (1.0 + lax.erf(uv * (2.0 ** -0.5)))
    u = uv[:, :SGU_WIDTH]
    v = _layer_norm(uv[:, SGU_WIDTH:], sg_ref[...], sb_ref[...]).astype(BF16)
    row = lax.broadcasted_iota(jnp.int32, (SGU_CHUNK, SGU_CHUNK), 0)
    col = lax.broadcasted_iota(jnp.int32, (SGU_CHUNK, SGU_CHUNK), 1)
    tril = row >= col
    for g in range(SGU_GROUPS):
        w = jnp.where(tril, ws_ref[g], 0.0).astype(BF16)
        gs = slice(g * SGU_GROUP_DIM, (g + 1) * SGU_GROUP_DIM)
        for cidx in range(tm // SGU_CHUNK):
            rs = slice(cidx * SGU_CHUNK, (cidx + 1) * SGU_CHUNK)
            sp = _dot(w, v[rs, gs]) + bs_ref[:, gs]
            gated_ref[rs, gs] = (u[rs, gs] * sp).astype(BF16)
    m = _dot(gated_ref[...], wo_ref[...])
    y = DEEPNORM_ALPHA * x + (1.0 + gate) * m
    o_ref[...] = _layer_norm(y, g_ref[1:2, :], b_ref[1:2, :])


def _sgu(x2, mod3, w_uv, b_uv, sgu_g, sgu_b, w_s, b_s_x, w_out, ln_g, ln_b, *, seq, tm):
    t = x2.shape[0]
    tiles_per_seq = seq // tm
    row = lambda i: (i, 0)
    return pl.pallas_call(
        functools.partial(_sgu_kernel, tm=tm),
        grid=(t // tm,),
        in_specs=[pl.BlockSpec((tm, D_MODEL), row),
                  pl.BlockSpec((1, N_MOD, D_MODEL), lambda i: (i // tiles_per_seq, 0, 0)),
                  _resident(w_uv.shape), _resident(b_uv.shape), _resident(sgu_g.shape),
                  _resident(sgu_b.shape), _resident(w_s.shape), _resident(b_s_x.shape),
                  _resident(w_out.shape), _resident(ln_g.shape), _resident(ln_b.shape)],
        out_specs=pl.BlockSpec((tm, D_MODEL), row),
        out_shape=jax.ShapeDtypeStruct((t, D_MODEL), F32),
        scratch_shapes=[pltpu.VMEM((tm, SGU_WIDTH), BF16)],
        compiler_params=_params(1),
        name="sgu_mixer",
    )(x2, mod3, w_uv, b_uv, sgu_g, sgu_b, w_s, b_s_x, w_out, ln_g, ln_b)


def _pad_cols(w, n):
    return jnp.pad(w, ((0, 0), (0, n - w.shape[1])))


def _rope_rot_cols(w):
    half = MLA_ROPE // 2
    return jnp.concatenate([-w[..., half:], w[..., :half]], axis=-1)


def _even_proj_weight(w_in):
    z, xbc, dt, cq, ckv, kr = jnp.split(
        w_in, [_OFF_XBC, _OFF_XBC + SSD_CONV_DIM, _OFF_XBC + SSD_CONV_DIM + SSD_HEADS,
               _OFF_XBC + SSD_CONV_DIM + SSD_HEADS + MLA_Q_RANK,
               _OFF_XBC + SSD_CONV_DIM + SSD_HEADS + MLA_Q_RANK + MLA_KV_RANK], axis=1)
    return jnp.concatenate([z, xbc, _pad_cols(dt, LANES), cq, ckv, _pad_cols(kr, LANES),
                            _pad_cols(_rope_rot_cols(kr), LANES)], axis=1).astype(BF16)


def _q_weight(w_uq):
    w = w_uq.reshape(MLA_Q_RANK, MLA_HEADS, MLA_NOPE + MLA_ROPE)
    nope = w[:, :, :MLA_NOPE].reshape(MLA_Q_RANK, -1)
    pe = w[:, :, MLA_NOPE:]
    pad = ((0, 0), (0, 0), (0, LANES - MLA_ROPE))
    pe_p = jnp.pad(pe, pad).reshape(MLA_Q_RANK, -1)
    rot_p = jnp.pad(_rope_rot_cols(pe), pad).reshape(MLA_Q_RANK, -1)
    return jnp.concatenate([nope, pe_p, rot_p], axis=1).astype(BF16)


def _kv_weight(w_ukv):
    w = w_ukv.reshape(MLA_KV_RANK, MLA_HEADS, MLA_NOPE + MLA_V)
    return jnp.concatenate([w[:, :, :MLA_NOPE].reshape(MLA_KV_RANK, -1),
                            w[:, :, MLA_NOPE:].reshape(MLA_KV_RANK, -1)], axis=1).astype(BF16)


def _rope_tables(seq):
    inv = 1.0 / (ROPE_THETA ** (jnp.arange(0, MLA_ROPE, 2, dtype=F32) / MLA_ROPE))
    ang = jnp.arange(seq, dtype=F32)[:, None] * inv[None, :]
    zeros = jnp.zeros((seq, LANES - MLA_ROPE), F32)
    cos, sin = jnp.cos(ang), jnp.sin(ang)
    return (jnp.concatenate([cos, cos, zeros], axis=1), jnp.concatenate([sin, sin, zeros], axis=1))


def _pad_lanes_row(v):
    return jnp.pad(v, (0, LANES - v.shape[0])).reshape(1, LANES)


def kernel(x, c, l0_ada_w, l0_ada_b, l0_ln_g, l0_ln_b, l0_ffa_w_in, l0_ffa_w_out, l0_ffb_w_in, l0_ffb_w_out, l0_w_in, l0_conv_w, l0_conv_b, l0_dt_bias, l0_a_log, l0_d_skip, l0_ssd_norm_w, l0_q_norm_w, l0_w_uq, l0_kv_norm_w, l0_w_ukv, l0_w_out, l1_ada_w, l1_ada_b, l1_ln_g, l1_ln_b, l1_ffa_w_in, l1_ffa_w_out, l1_ffb_w_in, l1_ffb_w_out, l1_w_uv, l1_b_uv, l1_sgu_ln_g, l1_sgu_ln_b, l1_w_s, l1_b_s, l1_w_out):
    batch, seq, d = x.shape
    assert d == D_MODEL and seq % SSD_CHUNK == 0
    tm = math.gcd(seq, 512)
    tq = math.gcd(seq, 256)
    x2 = x.reshape(batch * seq, d)
    bf = lambda w: w.astype(BF16)

    mod = _ada(c, l0_ada_w, l0_ada_b).reshape(batch, N_MOD, d)
    x2 = _ffn(x2, mod, bf(l0_ffa_w_in), bf(l0_ffa_w_out), l0_ln_g, l0_ln_b, sub=0, seq=seq, tm=tm)
    cos_t, sin_t = _rope_tables(seq)
    z, xbc, dt, q, k, v = _even_proj(
        x2, mod, _even_proj_weight(l0_w_in), l0_q_norm_w.reshape(1, -1), _q_weight(l0_w_uq),
        l0_kv_norm_w.reshape(1, -1), _kv_weight(l0_w_ukv), cos_t, sin_t, seq=seq, tm=tm)
    y_ssd = _ssd(xbc, dt, z, l0_conv_w, l0_conv_b.reshape(1, -1), _pad_lanes_row(l0_dt_bias),
                 _pad_lanes_row(l0_a_log), jnp.repeat(l0_d_skip, SSD_HEAD_DIM).reshape(1, -1),
                 l0_ssd_norm_w.reshape(1, -1), batch=batch, seq=seq)
    y_att = _attention(q, k, v, batch=batch, seq=seq, tq=tq)
    x2 = _even_out(x2, mod, y_ssd, y_att, bf(l0_w_out), l0_ln_g, l0_ln_b, seq=seq, tm=tm)
    x2 = _ffn(x2, mod, bf(l0_ffb_w_in), bf(l0_ffb_w_out), l0_ln_g, l0_ln_b, sub=2, seq=seq, tm=tm)

    mod = _ada(c, l1_ada_w, l1_ada_b).reshape(batch, N_MOD, d)
    x2 = _ffn(x2, mod, bf(l1_ffa_w_in), bf(l1_ffa_w_out), l1_ln_g, l1_ln_b, sub=0, seq=seq, tm=tm)
    b_s_x = jnp.repeat(l1_b_s.T, SGU_GROUP_DIM, axis=1)
    x2 = _sgu(x2, mod, bf(l1_w_uv), l1_b_uv.reshape(1, -1), l1_sgu_ln_g.reshape(1, -1),
              l1_sgu_ln_b.reshape(1, -1), l1_w_s, b_s_x, bf(l1_w_out), l1_ln_g, l1_ln_b,
              seq=seq, tm=math.gcd(seq, 256))
    x2 = _ffn(x2, mod, bf(l1_ffb_w_in), bf(l1_ffb_w_out), l1_ln_g, l1_ln_b, sub=2, seq=seq, tm=tm)
    return x2.reshape(batch, seq, d)
```

```python
import functools
import math

import jax
import jax.numpy as jnp
from jax import lax
from jax.experimental import pallas as pl
from jax.experimental.pallas import tpu as pltpu

F32 = jnp.float32
BF16 = jnp.bfloat16

D_MODEL = 1024
DEPTH = 2
D_FF = 2816
SSD_HEADS = 16
SSD_HEAD_DIM = 64
SSD_D_INNER = SSD_HEADS * SSD_HEAD_DIM
SSD_GROUPS = 2
SSD_STATE = 128
SSD_CONV = 4
SSD_CHUNK = 128
SSD_CONV_DIM = SSD_D_INNER + 2 * SSD_GROUPS * SSD_STATE
MLA_HEADS = 8
MLA_Q_RANK = 384
MLA_KV_RANK = 256
MLA_NOPE = 128
MLA_ROPE = 64
MLA_V = 128
ROPE_THETA = 10000.0
SGU_WIDTH = 2 * D_MODEL
SGU_GROUPS = 16
SGU_GROUP_DIM = SGU_WIDTH // SGU_GROUPS
SGU_CHUNK = 128
DEEPNORM_ALPHA = (2 * DEPTH) ** 0.25
N_MOD = 9
EPS = 1e-5

LANES = 128
SUBLANES = 8
VMEM_BYTES_V7X = 64 * 1024 * 1024
VMEM_LIMIT = (VMEM_BYTES_V7X * 3) // 4

QK_SLAB = MLA_NOPE + LANES

_OFF_Z = 0
_OFF_XBC = _OFF_Z + SSD_D_INNER
_OFF_DT = _OFF_XBC + SSD_CONV_DIM
_OFF_CQ = _OFF_DT + LANES
_OFF_CKV = _OFF_CQ + MLA_Q_RANK
_OFF_KPE = _OFF_CKV + MLA_KV_RANK
_OFF_KPE_ROT = _OFF_KPE + LANES
_PROJ_COLS = _OFF_KPE_ROT + LANES
_P_CQ = _OFF_CQ - _OFF_DT
_P_CKV = _OFF_CKV - _OFF_DT
_P_KPE = _OFF_KPE - _OFF_DT
_P_KPE_ROT = _OFF_KPE_ROT - _OFF_DT


def _silu(x):
    return x * jax.nn.sigmoid(x)


def _layer_norm(y, g, b):
    mu = jnp.mean(y, axis=-1, keepdims=True)
    yc = y - mu
    var = jnp.mean(yc * yc, axis=-1, keepdims=True)
    return yc * lax.rsqrt(var + EPS) * g + b


def _rms_norm(y, w):
    return y * lax.rsqrt(jnp.mean(y * y, axis=-1, keepdims=True) + EPS) * w


def _dot(a, b):
    return jnp.dot(a, b, preferred_element_type=F32)


def _dot_nt(a, b):
    return lax.dot_general(a, b, (((1,), (1,)), ((), ())), preferred_element_type=F32)


def _dot_tn(a, b):
    return lax.dot_general(a, b, (((0,), (0,)), ((), ())), preferred_element_type=F32)


def _split_bf16(x, terms):
    parts = []
    r = x
    for _ in range(terms):
        p = r.astype(BF16)
        parts.append(p)
        r = r - p.astype(F32)
    return parts


def _dot_split(a_bf16, x, terms):
    acc = None
    for p in _split_bf16(x, terms):
        t = _dot(a_bf16, p)
        acc = t if acc is None else acc + t
    return acc


def _dot_split_rhs(x, b_bf16, terms):
    acc = None
    for p in _split_bf16(x, terms):
        t = _dot(p, b_bf16)
        acc = t if acc is None else acc + t
    return acc


def _resident(shape):
    n = len(shape)
    return pl.BlockSpec(shape, lambda *_: (0,) * n, pipeline_mode=pl.Buffered(1))


def _params(n_axes):
    return pltpu.CompilerParams(dimension_semantics=("arbitrary",) * n_axes,
                                vmem_limit_bytes=VMEM_LIMIT)


def _ada_kernel(c_ref, w_ref, b_ref, o_ref):
    s = _silu(c_ref[...]).astype(BF16)
    o_ref[...] = _dot(s, w_ref[...].astype(BF16)) + b_ref[...]


def _ada(c, ada_w, ada_b):
    b, d = c.shape
    n = ada_w.shape[1]
    tn = D_MODEL
    return pl.pallas_call(
        _ada_kernel,
        grid=(n // tn,),
        in_specs=[pl.BlockSpec((b, d), lambda j: (0, 0)),
                  pl.BlockSpec((d, tn), lambda j: (0, j)),
                  pl.BlockSpec((1, tn), lambda j: (0, j))],
        out_specs=pl.BlockSpec((b, tn), lambda j: (0, j)),
        out_shape=jax.ShapeDtypeStruct((b, n), F32),
        compiler_params=_params(1),
        name="ada_mod",
    )(c, ada_w, ada_b.reshape(1, n))


def _mod_rows(mod_ref, k):
    return (mod_ref[0, 3 * k:3 * k + 1, :], mod_ref[0, 3 * k + 1:3 * k + 2, :],
            mod_ref[0, 3 * k + 2:3 * k + 3, :])


def _ffn_kernel(x_ref, mod_ref, win_ref, wout_ref, g_ref, b_ref, o_ref, *, sub):
    x = x_ref[...]
    sh, sc, gate = _mod_rows(mod_ref, sub)
    h = (x * (1.0 + sc) + sh).astype(BF16)
    ab = _dot(h, win_ref[...])
    a = ab[:, :D_FF]
    b = ab[:, D_FF:]
    act = (_silu(a) * b).astype(BF16)
    f = _dot(act, wout_ref[...])
    y = DEEPNORM_ALPHA * x + (1.0 + gate) * (0.5 * f)
    o_ref[...] = _layer_norm(y, g_ref[sub:sub + 1, :], b_ref[sub:sub + 1, :])


def _ffn(x2, mod3, w_in, w_out, ln_g, ln_b, *, sub, seq, tm):
    t = x2.shape[0]
    tiles_per_seq = seq // tm
    return pl.pallas_call(
        functools.partial(_ffn_kernel, sub=sub),
        grid=(t // tm,),
        in_specs=[pl.BlockSpec((tm, D_MODEL), lambda i: (i, 0)),
                  pl.BlockSpec((1, N_MOD, D_MODEL), lambda i: (i // tiles_per_seq, 0, 0)),
                  _resident(w_in.shape), _resident(w_out.shape),
                  _resident(ln_g.shape), _resident(ln_b.shape)],
        out_specs=pl.BlockSpec((tm, D_MODEL), lambda i: (i, 0)),
        out_shape=jax.ShapeDtypeStruct((t, D_MODEL), F32),
        compiler_params=_params(1),
        name="ffn_sublayer",
    )(x2, mod3, w_in, w_out, ln_g, ln_b)


def _even_proj_kernel(x_ref, mod_ref, wc_ref, convw_ref, convb_ref, qnw_ref, wq_ref, kvnw_ref, wkv_ref,
                      cos_ref, sin_ref, z_ref, xs_ref, bc_ref, dt_ref, q_ref, k_ref, v_ref, xwin_ref,
                      *, tm, tiles_per_seq):
    i = pl.program_id(0)

    @pl.when(i == 0)
    def _():
        xwin_ref[tm:tm + SUBLANES, :] = jnp.zeros((SUBLANES, SSD_CONV_DIM), F32)

    x = x_ref[...]
    sh, sc, _ = _mod_rows(mod_ref, 1)
    h = (x * (1.0 + sc) + sh).astype(BF16)
    seq_start = (i % tiles_per_seq) == 0
    xwin_ref[0:SUBLANES, :] = jnp.where(seq_start, 0.0, xwin_ref[tm:tm + SUBLANES, :])
    xwin_ref[SUBLANES:SUBLANES + tm, :] = _dot(h, wc_ref[:, _OFF_XBC:_OFF_DT])
    z_ref[...] = _dot(h, wc_ref[:, _OFF_Z:_OFF_XBC])
    p = _dot(h, wc_ref[:, _OFF_DT:])
    dt_ref[...] = p[:, 0:LANES]
    conv = convb_ref[...]
    for k in range(SSD_CONV):
        lo = SUBLANES - (SSD_CONV - 1) + k
        conv = conv + convw_ref[k:k + 1, :] * xwin_ref[lo:lo + tm, :]
    xbc = _silu(conv)
    xs_ref[...] = xbc[:, :SSD_D_INNER]
    bc_ref[...] = xbc[:, SSD_D_INNER:].astype(BF16)

    cos = cos_ref[...]
    sin = sin_ref[...]
    k_pe = (p[:, _P_KPE:_P_KPE_ROT] * cos + p[:, _P_KPE_ROT:] * sin).astype(BF16)

    cqn = _rms_norm(p[:, _P_CQ:_P_CKV], qnw_ref[...]).astype(BF16)
    q = _dot(cqn, wq_ref[...])
    scale = (MLA_NOPE + MLA_ROPE) ** -0.5
    hw = MLA_HEADS * LANES
    for hh in range(MLA_HEADS):
        lo = hh * LANES
        q_nope = q[:, lo:lo + LANES]
        q_pe = q[:, hw + lo:hw + lo + LANES] * cos + q[:, 2 * hw + lo:2 * hw + lo + LANES] * sin
        q_ref[:, hh * QK_SLAB:hh * QK_SLAB + MLA_NOPE] = (q_nope * scale).astype(BF16)
        q_ref[:, hh * QK_SLAB + MLA_NOPE:(hh + 1) * QK_SLAB] = (q_pe * scale).astype(BF16)

    ckvn = _rms_norm(p[:, _P_CKV:_P_KPE], kvnw_ref[...]).astype(BF16)
    kv = _dot(ckvn, wkv_ref[...])
    for hh in range(MLA_HEADS):
        k_ref[:, hh * QK_SLAB:hh * QK_SLAB + MLA_NOPE] = kv[:, hh * LANES:(hh + 1) * LANES].astype(BF16)
        k_ref[:, hh * QK_SLAB + MLA_NOPE:(hh + 1) * QK_SLAB] = k_pe
    v_ref[...] = kv[:, hw:].astype(BF16)


def _even_proj(x2, mod3, wc, conv_w, conv_b, qnw, wq, kvnw, wkv, cos_t, sin_t, *, seq, tm):
    t = x2.shape[0]
    tiles_per_seq = seq // tm
    row = lambda i: (i, 0)
    pos = lambda i: (i % tiles_per_seq, 0)
    out_shapes = (jax.ShapeDtypeStruct((t, SSD_D_INNER), F32),
                  jax.ShapeDtypeStruct((t, SSD_D_INNER), F32),
                  jax.ShapeDtypeStruct((t, 2 * SSD_GROUPS * SSD_STATE), BF16),
                  jax.ShapeDtypeStruct((t, LANES), F32),
                  jax.ShapeDtypeStruct((t, MLA_HEADS * QK_SLAB), BF16),
                  jax.ShapeDtypeStruct((t, MLA_HEADS * QK_SLAB), BF16),
                  jax.ShapeDtypeStruct((t, MLA_HEADS * MLA_V), BF16))
    return pl.pallas_call(
        functools.partial(_even_proj_kernel, tm=tm, tiles_per_seq=tiles_per_seq),
        grid=(t // tm,),
        in_specs=[pl.BlockSpec((tm, D_MODEL), row),
                  pl.BlockSpec((1, N_MOD, D_MODEL), lambda i: (i // tiles_per_seq, 0, 0)),
                  _resident(wc.shape), _resident(conv_w.shape), _resident(conv_b.shape),
                  _resident(qnw.shape), _resident(wq.shape),
                  _resident(kvnw.shape), _resident(wkv.shape),
                  pl.BlockSpec((tm, LANES), pos), pl.BlockSpec((tm, LANES), pos)],
        out_specs=tuple(pl.BlockSpec((tm, s.shape[1]), row) for s in out_shapes),
        out_shape=out_shapes,
        scratch_shapes=[pltpu.VMEM((SUBLANES + tm, SSD_CONV_DIM), F32)],
        compiler_params=_params(1),
        name="even_proj",
    )(x2, mod3, wc, conv_w, conv_b, qnw, wq, kvnw, wkv, cos_t, sin_t)


def _ssd_kernel(xs_ref, bc_ref, dt_ref, z_ref, dtb_ref, alog_ref, dskip_ref, nw_ref, o_ref, state_ref):
    q = SSD_CHUNK
    c = pl.program_id(1)

    @pl.when(c == 0)
    def _():
        state_ref[...] = jnp.zeros_like(state_ref)

    xs = xs_ref[...]
    gn = SSD_GROUPS * SSD_STATE
    bm_b = bc_ref[:, :gn]
    cm_b = bc_ref[:, gn:]

    lane = lax.broadcasted_iota(jnp.int32, (1, LANES), 1)
    head_lane = lane < SSD_HEADS
    dt = jax.nn.softplus(dt_ref[...] + dtb_ref[...])
    a = jnp.where(head_lane, -jnp.exp(alog_ref[...]), 0.0)
    adt = dt * a
    row = lax.broadcasted_iota(jnp.int32, (q, q), 0)
    col = lax.broadcasted_iota(jnp.int32, (q, q), 1)
    tril = row >= col
    tril_b = jnp.where(tril, 1.0, 0.0).astype(BF16)
    cs = _dot_split(tril_b, adt, 3)
    cs_t = cs.T
    cs_last = cs[q - 1:q, :]
    ecs = jnp.exp(cs)
    dte = jnp.exp(cs_last - cs)

    er = lax.broadcasted_iota(jnp.int32, (LANES, SSD_D_INNER), 0)
    ec = lax.broadcasted_iota(jnp.int32, (LANES, SSD_D_INNER), 1)
    expand = jnp.where(ec // SSD_HEAD_DIM == er, 1.0, 0.0).astype(BF16)
    dt_x = _dot(dt.astype(BF16), expand)
    ecs_x = _dot_split_rhs(ecs, expand, 2)
    dte_x = _dot(dte.astype(BF16), expand)

    xc = xs * dt_x
    xc_b = xc.astype(BF16)
    xd_b = (xc * dte_x).astype(BF16)
    first_head = col < SSD_HEAD_DIM
    pairs_per_group = SSD_HEADS // SSD_GROUPS // 2
    ys = []
    for g in range(SSD_GROUPS):
        bg = bm_b[:, g * SSD_STATE:(g + 1) * SSD_STATE]
        cg = cm_b[:, g * SSD_STATE:(g + 1) * SSD_STATE]
        cb = _dot_nt(cg, bg)
        for r in range(pairs_per_group):
            j = g * pairs_per_group + r
            ps = slice(j * LANES, (j + 1) * LANES)
            ws = []
            for hd in (2 * j, 2 * j + 1):
                diff = cs[:, hd:hd + 1] - cs_t[hd:hd + 1, :]
                decay = jnp.where(tril, jnp.exp(jnp.where(tril, diff, 0.0)), 0.0)
                ws.append((cb * decay).astype(BF16))
            xp = xc_b[:, ps]
            zero = jnp.zeros_like(xp)
            y_diag = _dot(jnp.concatenate(ws, axis=1),
                          jnp.concatenate([jnp.where(first_head, xp, zero),
                                           jnp.where(first_head, zero, xp)], axis=0))
            st = state_ref[j]
            ys.append(y_diag + ecs_x[:, ps] * _dot(cg, st.astype(BF16)))
            state_ref[j] = ecs_x[q - 1:q, ps] * st + _dot_tn(bg, xd_b[:, ps])

    y = jnp.concatenate(ys, axis=1) + dskip_ref[...] * xs
    y = y * _silu(z_ref[...])
    gw = SSD_D_INNER // SSD_GROUPS
    for g in range(SSD_GROUPS):
        sl = slice(g * gw, (g + 1) * gw)
        o_ref[:, sl] = _rms_norm(y[:, sl], nw_ref[:, sl]).astype(BF16)


def _ssd(xs, bc, dt, z, dt_bias, a_log, d_skip_x, norm_w, *, batch, seq):
    q = SSD_CHUNK
    nc = seq // q
    t = batch * seq
    row = lambda b, c: (b * nc + c, 0)
    return pl.pallas_call(
        _ssd_kernel,
        grid=(batch, nc),
        in_specs=[pl.BlockSpec((q, SSD_D_INNER), row),
                  pl.BlockSpec((q, 2 * SSD_GROUPS * SSD_STATE), row),
                  pl.BlockSpec((q, LANES), row),
                  pl.BlockSpec((q, SSD_D_INNER), row),
                  _resident(dt_bias.shape), _resident(a_log.shape), _resident(d_skip_x.shape),
                  _resident(norm_w.shape)],
        out_specs=pl.BlockSpec((q, SSD_D_INNER), row),
        out_shape=jax.ShapeDtypeStruct((t, SSD_D_INNER), BF16),
        scratch_shapes=[pltpu.VMEM((SSD_HEADS // 2, SSD_STATE, 2 * SSD_HEAD_DIM), F32)],
        compiler_params=_params(2),
        name="ssd_scan",
    )(xs, bc, dt, z, dt_bias, a_log, d_skip_x, norm_w)


def _attn_kernel(q_ref, k_ref, v_ref, o_ref, *, tq):
    seq = q_ref.shape[0]
    row = lax.broadcasted_iota(jnp.int32, (tq, tq), 0)
    col = lax.broadcasted_iota(jnp.int32, (tq, tq), 1)
    causal = row >= col

    def scores(i):
        off = i * tq
        qt = q_ref[off:off + tq, :]
        s_d = jnp.where(causal, _dot_nt(qt, k_ref[off:off + tq, :]), -jnp.inf)
        s_o = _dot_nt(qt, k_ref[0:off, :]) if off else None
        return s_d, s_o

    n = seq // tq
    nxt = scores(0)
    for i in range(n):
        off = i * tq
        s_d, s_o = nxt
        if i + 1 < n:
            nxt = scores(i + 1)
        m = jnp.max(s_d, axis=-1, keepdims=True)
        if off:
            m = jnp.maximum(m, jnp.max(s_o, axis=-1, keepdims=True))
        p_d = jnp.exp(s_d - m)
        l = jnp.sum(p_d, axis=-1, keepdims=True)
        acc = _dot(p_d.astype(BF16), v_ref[off:off + tq, :])
        if off:
            p_o = jnp.exp(s_o - m)
            l = l + jnp.sum(p_o, axis=-1, keepdims=True)
            acc = acc + _dot(p_o.astype(BF16), v_ref[0:off, :])
        o_ref[off:off + tq, :] = (acc / l).astype(BF16)


def _attention(q, k, v, *, batch, seq, tq):
    t = batch * seq
    return pl.pallas_call(
        functools.partial(_attn_kernel, tq=tq),
        grid=(batch, MLA_HEADS),
        in_specs=[pl.BlockSpec((seq, QK_SLAB), lambda b, h: (b, h)),
                  pl.BlockSpec((seq, QK_SLAB), lambda b, h: (b, h)),
                  pl.BlockSpec((seq, MLA_V), lambda b, h: (b, h))],
        out_specs=pl.BlockSpec((seq, MLA_V), lambda b, h: (b, h)),
        out_shape=jax.ShapeDtypeStruct((t, MLA_HEADS * MLA_V), BF16),
        compiler_params=_params(2),
        name="mla_attention",
    )(q, k, v)


def _even_out_kernel(x_ref, mod_ref, ys_ref, ya_ref, wo_ref, g_ref, b_ref, o_ref):
    x = x_ref[...]
    _, _, gate = _mod_rows(mod_ref, 1)
    m = _dot(ys_ref[...], wo_ref[0:SSD_D_INNER, :]) + _dot(ya_ref[...], wo_ref[SSD_D_INNER:, :])
    y = DEEPNORM_ALPHA * x + (1.0 + gate) * m
    o_ref[...] = _layer_norm(y, g_ref[1:2, :], b_ref[1:2, :])


def _even_out(x2, mod3, y_ssd, y_att, w_out, ln_g, ln_b, *, seq, tm):
    t = x2.shape[0]
    tiles_per_seq = seq // tm
    row = lambda i: (i, 0)
    return pl.pallas_call(
        _even_out_kernel,
        grid=(t // tm,),
        in_specs=[pl.BlockSpec((tm, D_MODEL), row),
                  pl.BlockSpec((1, N_MOD, D_MODEL), lambda i: (i // tiles_per_seq, 0, 0)),
                  pl.BlockSpec((tm, SSD_D_INNER), row),
                  pl.BlockSpec((tm, MLA_HEADS * MLA_V), row),
                  _resident(w_out.shape), _resident(ln_g.shape), _resident(ln_b.shape)],
        out_specs=pl.BlockSpec((tm, D_MODEL), row),
        out_shape=jax.ShapeDtypeStruct((t, D_MODEL), F32),
        compiler_params=_params(1),
        name="even_out",
    )(x2, mod3, y_ssd, y_att, w_out, ln_g, ln_b)


def _sgu_kernel(x_ref, mod_ref, wuv_ref, buv_ref, sg_ref, sb_ref, ws_ref, bs_ref, wo_ref, g_ref, b_ref,
                o_ref, gated_ref, *, tm):
    x = x_ref[...]
    sh, sc, gate = _mod_rows(mod_ref, 1)
    h = (x * (1.0 + sc) + sh).astype(BF16)
    gelu = lambda t: 0.5 * t * (1.0 + lax.erf(t * (2.0 ** -0.5)))
    v = gelu(_dot(h, wuv_ref[:, SGU_WIDTH:]) + buv_ref[:, SGU_WIDTH:])
    v = _layer_norm(v, sg_ref[...], sb_ref[...]).astype(BF16)
    u = gelu(_dot(h, wuv_ref[:, :SGU_WIDTH]) + buv_ref[:, :SGU_WIDTH])
    row = lax.broadcasted_iota(jnp.int32, (SGU_CHUNK, SGU_CHUNK), 0)
    col = lax.broadcasted_iota(jnp.int32, (SGU_CHUNK, SGU_CHUNK), 1)
    tril = row >= col
    for g in range(SGU_GROUPS):
        w = jnp.where(tril, ws_ref[g], 0.0).astype(BF16)
        gs = slice(g * SGU_GROUP_DIM, (g + 1) * SGU_GROUP_DIM)
        for cidx in range(tm // SGU_CHUNK):
            rs = slice(cidx * SGU_CHUNK, (cidx + 1) * SGU_CHUNK)
            sp = _dot(w, v[rs, gs]) + bs_ref[:, gs]
            gated_ref[rs, gs] = (u[rs, gs] * sp).astype(BF16)
    m = _dot(gated_ref[...], wo_ref[...])
    y = DEEPNORM_ALPHA * x + (1.0 + gate) * m
    o_ref[...] = _layer_norm(y, g_ref[1:2, :], b_ref[1:2, :])


def _sgu(x2, mod3, w_uv, b_uv, sgu_g, sgu_b, w_s, b_s_x, w_out, ln_g, ln_b, *, seq, tm):
    t = x2.shape[0]
    tiles_per_seq = seq // tm
    row = lambda i: (i, 0)
    return pl.pallas_call(
        functools.partial(_sgu_kernel, tm=tm),
        grid=(t // tm,),
        in_specs=[pl.BlockSpec((tm, D_MODEL), row),
                  pl.BlockSpec((1, N_MOD, D_MODEL), lambda i: (i // tiles_per_seq, 0, 0)),
                  _resident(w_uv.shape), _resident(b_uv.shape), _resident(sgu_g.shape),
                  _resident(sgu_b.shape), _resident(w_s.shape), _resident(b_s_x.shape),
                  _resident(w_out.shape), _resident(ln_g.shape), _resident(ln_b.shape)],
        out_specs=pl.BlockSpec((tm, D_MODEL), row),
        out_shape=jax.ShapeDtypeStruct((t, D_MODEL), F32),
        scratch_shapes=[pltpu.VMEM((tm, SGU_WIDTH), BF16)],
        compiler_params=_params(1),
        name="sgu_mixer",
    )(x2, mod3, w_uv, b_uv, sgu_g, sgu_b, w_s, b_s_x, w_out, ln_g, ln_b)


def _pad_cols(w, n):
    return jnp.pad(w, ((0, 0), (0, n - w.shape[1])))


def _rope_rot_cols(w):
    half = MLA_ROPE // 2
    return jnp.concatenate([-w[..., half:], w[..., :half]], axis=-1)


def _even_proj_weight(w_in):
    z, xbc, dt, cq, ckv, kr = jnp.split(
        w_in, [_OFF_XBC, _OFF_XBC + SSD_CONV_DIM, _OFF_XBC + SSD_CONV_DIM + SSD_HEADS,
               _OFF_XBC + SSD_CONV_DIM + SSD_HEADS + MLA_Q_RANK,
               _OFF_XBC + SSD_CONV_DIM + SSD_HEADS + MLA_Q_RANK + MLA_KV_RANK], axis=1)
    return jnp.concatenate([z, xbc, _pad_cols(dt, LANES), cq, ckv, _pad_cols(kr, LANES),
                            _pad_cols(_rope_rot_cols(kr), LANES)], axis=1).astype(BF16)


def _q_weight(w_uq):
    w = w_uq.reshape(MLA_Q_RANK, MLA_HEADS, MLA_NOPE + MLA_ROPE)
    nope = w[:, :, :MLA_NOPE].reshape(MLA_Q_RANK, -1)
    pe = w[:, :, MLA_NOPE:]
    pad = ((0, 0), (0, 0), (0, LANES - MLA_ROPE))
    pe_p = jnp.pad(pe, pad).reshape(MLA_Q_RANK, -1)
    rot_p = jnp.pad(_rope_rot_cols(pe), pad).reshape(MLA_Q_RANK, -1)
    return jnp.concatenate([nope, pe_p, rot_p], axis=1).astype(BF16)


def _kv_weight(w_ukv):
    w = w_ukv.reshape(MLA_KV_RANK, MLA_HEADS, MLA_NOPE + MLA_V)
    return jnp.concatenate([w[:, :, :MLA_NOPE].reshape(MLA_KV_RANK, -1),
                            w[:, :, MLA_NOPE:].reshape(MLA_KV_RANK, -1)], axis=1).astype(BF16)


def _rope_tables(seq):
    inv = 1.0 / (ROPE_THETA ** (jnp.arange(0, MLA_ROPE, 2, dtype=F32) / MLA_ROPE))
    ang = jnp.arange(seq, dtype=F32)[:, None] * inv[None, :]
    zeros = jnp.zeros((seq, LANES - MLA_ROPE), F32)
    cos, sin = jnp.cos(ang), jnp.sin(ang)
    return (jnp.concatenate([cos, cos, zeros], axis=1), jnp.concatenate([sin, sin, zeros], axis=1))


def _pad_lanes_row(v):
    return jnp.pad(v, (0, LANES - v.shape[0])).reshape(1, LANES)


def kernel(x, c, l0_ada_w, l0_ada_b, l0_ln_g, l0_ln_b, l0_ffa_w_in, l0_ffa_w_out, l0_ffb_w_in, l0_ffb_w_out, l0_w_in, l0_conv_w, l0_conv_b, l0_dt_bias, l0_a_log, l0_d_skip, l0_ssd_norm_w, l0_q_norm_w, l0_w_uq, l0_kv_norm_w, l0_w_ukv, l0_w_out, l1_ada_w, l1_ada_b, l1_ln_g, l1_ln_b, l1_ffa_w_in, l1_ffa_w_out, l1_ffb_w_in, l1_ffb_w_out, l1_w_uv, l1_b_uv, l1_sgu_ln_g, l1_sgu_ln_b, l1_w_s, l1_b_s, l1_w_out):
    batch, seq, d = x.shape
    assert d == D_MODEL and seq % SSD_CHUNK == 0
    tm = math.gcd(seq, 512)
    tq = math.gcd(seq, 256)
    x2 = x.reshape(batch * seq, d)
    bf = lambda w: w.astype(BF16)

    mod = _ada(c, l0_ada_w, l0_ada_b).reshape(batch, N_MOD, d)
    x2 = _ffn(x2, mod, bf(l0_ffa_w_in), bf(l0_ffa_w_out), l0_ln_g, l0_ln_b, sub=0, seq=seq, tm=tm)
    cos_t, sin_t = _rope_tables(seq)
    z, xs, bc, dt, q, k, v = _even_proj(
        x2, mod, _even_proj_weight(l0_w_in), l0_conv_w, l0_conv_b.reshape(1, -1),
        l0_q_norm_w.reshape(1, -1), _q_weight(l0_w_uq),
        l0_kv_norm_w.reshape(1, -1), _kv_weight(l0_w_ukv), cos_t, sin_t, seq=seq, tm=tm)
    y_ssd = _ssd(xs, bc, dt, z, _pad_lanes_row(l0_dt_bias),
                 _pad_lanes_row(l0_a_log), jnp.repeat(l0_d_skip, SSD_HEAD_DIM).reshape(1, -1),
                 l0_ssd_norm_w.reshape(1, -1), batch=batch, seq=seq)
    y_att = _attention(q, k, v, batch=batch, seq=seq, tq=tq)
    x2 = _even_out(x2, mod, y_ssd, y_att, bf(l0_w_out), l0_ln_g, l0_ln_b, seq=seq, tm=tm)
    x2 = _ffn(x2, mod, bf(l0_ffb_w_in), bf(l0_ffb_w_out), l0_ln_g, l0_ln_b, sub=2, seq=seq, tm=tm)

    mod = _ada(c, l1_ada_w, l1_ada_b).reshape(batch, N_MOD, d)
    x2 = _ffn(x2, mod, bf(l1_ffa_w_in), bf(l1_ffa_w_out), l1_ln_g, l1_ln_b, sub=0, seq=seq, tm=tm)
    b_s_x = jnp.repeat(l1_b_s.T, SGU_GROUP_DIM, axis=1)
    x2 = _sgu(x2, mod, bf(l1_w_uv), l1_b_uv.reshape(1, -1), l1_sgu_ln_g.reshape(1, -1),
              l1_sgu_ln_b.reshape(1, -1), l1_w_s, b_s_x, bf(l1_w_out), l1_ln_g, l1_ln_b,
              seq=seq, tm=tm)
    x2 = _ffn(x2, mod, bf(l1_ffb_w_in), bf(l1_ffb_w_out), l1_ln_g, l1_ln_b, sub=2, seq=seq, tm=tm)
    return x2.reshape(batch, seq, d)
```

```python
import functools
import math

import jax
import jax.numpy as jnp
from jax import lax
from jax.experimental import pallas as pl
from jax.experimental.pallas import tpu as pltpu

F32 = jnp.float32
BF16 = jnp.bfloat16

D_MODEL = 1024
DEPTH = 2
D_FF = 2816
SSD_HEADS = 16
SSD_HEAD_DIM = 64
SSD_D_INNER = SSD_HEADS * SSD_HEAD_DIM
SSD_GROUPS = 2
SSD_STATE = 128
SSD_CONV = 4
SSD_CHUNK = 128
SSD_CONV_DIM = SSD_D_INNER + 2 * SSD_GROUPS * SSD_STATE
MLA_HEADS = 8
MLA_Q_RANK = 384
MLA_KV_RANK = 256
MLA_NOPE = 128
MLA_ROPE = 64
MLA_V = 128
ROPE_THETA = 10000.0
SGU_WIDTH = 2 * D_MODEL
SGU_GROUPS = 16
SGU_GROUP_DIM = SGU_WIDTH // SGU_GROUPS
SGU_CHUNK = 128
DEEPNORM_ALPHA = (2 * DEPTH) ** 0.25
N_MOD = 9
EPS = 1e-5
LOG2E = math.log2(math.e)

LANES = 128
SUBLANES = 8
VMEM_BYTES_V7X = 64 * 1024 * 1024
VMEM_LIMIT = (VMEM_BYTES_V7X * 3) // 4

QK_SLAB = MLA_NOPE + LANES

_OFF_Z = 0
_OFF_XBC = _OFF_Z + SSD_D_INNER
_OFF_DT = _OFF_XBC + SSD_CONV_DIM
_OFF_CQ = _OFF_DT + LANES
_OFF_CKV = _OFF_CQ + MLA_Q_RANK
_OFF_KPE = _OFF_CKV + MLA_KV_RANK
_OFF_KPE_ROT = _OFF_KPE + LANES
_PROJ_COLS = _OFF_KPE_ROT + LANES
_P_CQ = _OFF_CQ - _OFF_DT
_P_CKV = _OFF_CKV - _OFF_DT
_P_KPE = _OFF_KPE - _OFF_DT
_P_KPE_ROT = _OFF_KPE_ROT - _OFF_DT


def _silu(x):
    return x * jax.nn.sigmoid(x)


def _layer_norm(y, g, b):
    mu = jnp.mean(y, axis=-1, keepdims=True)
    yc = y - mu
    var = jnp.mean(yc * yc, axis=-1, keepdims=True)
    return yc * lax.rsqrt(var + EPS) * g + b


def _rms_norm(y, w):
    return y * lax.rsqrt(jnp.mean(y * y, axis=-1, keepdims=True) + EPS) * w


def _dot(a, b):
    return jnp.dot(a, b, preferred_element_type=F32)


def _dot_nt(a, b):
    return lax.dot_general(a, b, (((1,), (1,)), ((), ())), preferred_element_type=F32)


def _dot_tn(a, b):
    return lax.dot_general(a, b, (((0,), (0,)), ((), ())), preferred_element_type=F32)


def _split_bf16(x, terms):
    parts = []
    r = x
    for _ in range(terms):
        p = r.astype(BF16)
        parts.append(p)
        r = r - p.astype(F32)
    return parts


def _dot_split(a_bf16, x, terms):
    acc = None
    for p in _split_bf16(x, terms):
        t = _dot(a_bf16, p)
        acc = t if acc is None else acc + t
    return acc


def _dot_split_rhs(x, b_bf16, terms):
    acc = None
    for p in _split_bf16(x, terms):
        t = _dot(p, b_bf16)
        acc = t if acc is None else acc + t
    return acc


def _resident(shape):
    n = len(shape)
    return pl.BlockSpec(shape, lambda *_: (0,) * n, pipeline_mode=pl.Buffered(1))


def _cast_view(w, steps):
    rows_unit = steps * 2 * SUBLANES
    for cols in [w.shape[1]] + [LANES * m for m in range(64, 0, -1)]:
        if w.size % cols == 0 and (w.size // cols) % rows_unit == 0:
            return w.reshape(w.size // cols, cols)
    raise ValueError(f"no tile-aligned view of {w.shape} for {steps} steps")


def _cast_specs(views, steps):
    return [pl.BlockSpec((v.shape[0] // steps, v.shape[1]), lambda i: (i, 0)) for v in views]


def _cast_blocks(src_refs, dst_refs):
    for s_ref, d_ref in zip(src_refs, dst_refs):
        d_ref[...] = s_ref[...].astype(BF16)


def _params(n_axes):
    return pltpu.CompilerParams(dimension_semantics=("arbitrary",) * n_axes,
                                vmem_limit_bytes=VMEM_LIMIT)


def _ada_kernel(c_ref, w_ref, b_ref, o_ref):
    s = _silu(c_ref[...]).astype(BF16)
    o_ref[...] = _dot(s, w_ref[...].astype(BF16)) + b_ref[...]


def _ada(c, ada_w, ada_b):
    b, d = c.shape
    n = ada_w.shape[1]
    tn = D_MODEL
    return pl.pallas_call(
        _ada_kernel,
        grid=(n // tn,),
        in_specs=[pl.BlockSpec((b, d), lambda j: (0, 0)),
                  pl.BlockSpec((d, tn), lambda j: (0, j)),
                  pl.BlockSpec((1, tn), lambda j: (0, j))],
        out_specs=pl.BlockSpec((b, tn), lambda j: (0, j)),
        out_shape=jax.ShapeDtypeStruct((b, n), F32),
        compiler_params=_params(1),
        name="ada_mod",
    )(c, ada_w, ada_b.reshape(1, n))


def _mod_rows(mod_ref, k):
    return (mod_ref[0, 3 * k:3 * k + 1, :], mod_ref[0, 3 * k + 1:3 * k + 2, :],
            mod_ref[0, 3 * k + 2:3 * k + 3, :])


def _ffn_kernel(x_ref, mod_ref, win_ref, wout_ref, g_ref, b_ref, *rest, sub):
    n_cast = (len(rest) - 1) // 2
    o_ref = rest[n_cast]
    _cast_blocks(rest[:n_cast], rest[n_cast + 1:])
    x = x_ref[...]
    sh, sc, gate = _mod_rows(mod_ref, sub)
    h = (x * (1.0 + sc) + sh).astype(BF16)
    ab = _dot(h, win_ref[...])
    a = ab[:, :D_FF]
    b = ab[:, D_FF:]
    act = (_silu(a) * b).astype(BF16)
    f = _dot(act, wout_ref[...])
    y = DEEPNORM_ALPHA * x + (1.0 + gate) * (0.5 * f)
    o_ref[...] = _layer_norm(y, g_ref[sub:sub + 1, :], b_ref[sub:sub + 1, :])


def _ffn(x2, mod3, w_in, w_out, ln_g, ln_b, *, sub, seq, tm, cast=()):
    t = x2.shape[0]
    steps = t // tm
    tiles_per_seq = seq // tm
    views = [_cast_view(w, steps) for w in cast]
    cast_specs = _cast_specs(views, steps)
    outs = pl.pallas_call(
        functools.partial(_ffn_kernel, sub=sub),
        grid=(steps,),
        in_specs=[pl.BlockSpec((tm, D_MODEL), lambda i: (i, 0)),
                  pl.BlockSpec((1, N_MOD, D_MODEL), lambda i: (i // tiles_per_seq, 0, 0)),
                  _resident(w_in.shape), _resident(w_out.shape),
                  _resident(ln_g.shape), _resident(ln_b.shape)] + cast_specs,
        out_specs=[pl.BlockSpec((tm, D_MODEL), lambda i: (i, 0))] + cast_specs,
        out_shape=[jax.ShapeDtypeStruct((t, D_MODEL), F32)]
        + [jax.ShapeDtypeStruct(v.shape, BF16) for v in views],
        compiler_params=_params(1),
        name="ffn_sublayer",
    )(x2, mod3, w_in, w_out, ln_g, ln_b, *views)
    return outs[0], [o.reshape(w.shape) for o, w in zip(outs[1:], cast)]


def _even_proj_kernel(x_ref, mod_ref, wc_ref, convw_ref, convb_ref, qnw_ref, wq_ref, kvnw_ref, wkv_ref,
                      cos_ref, sin_ref, z_ref, xs_ref, bc_ref, dt_ref, q_ref, k_ref, v_ref, tail_ref,
                      *, tm, tiles_per_seq):
    i = pl.program_id(0)

    @pl.when(i == 0)
    def _():
        tail_ref[...] = jnp.zeros((SUBLANES, SSD_CONV_DIM), F32)

    x = x_ref[...]
    sh, sc, _ = _mod_rows(mod_ref, 1)
    h = (x * (1.0 + sc) + sh).astype(BF16)
    seq_start = (i % tiles_per_seq) == 0
    tail = jnp.where(seq_start, 0.0, tail_ref[...])
    xr = _dot(h, wc_ref[:, _OFF_XBC:_OFF_DT])
    tail_ref[...] = xr[tm - SUBLANES:, :]
    z_ref[...] = _dot(h, wc_ref[:, _OFF_Z:_OFF_XBC])
    p = _dot(h, wc_ref[:, _OFF_DT:])
    dt_ref[...] = p[:, 0:LANES]
    sub_id = lax.broadcasted_iota(jnp.int32, (SUBLANES, SSD_CONV_DIM), 0)
    conv = convb_ref[...] + convw_ref[SSD_CONV - 1:SSD_CONV, :] * xr
    for j in range(1, SSD_CONV):
        rolled = pltpu.roll(xr, j, axis=0)
        head = jnp.where(sub_id < j, pltpu.roll(tail, j, axis=0), rolled[0:SUBLANES, :])
        shifted = jnp.concatenate([head, rolled[SUBLANES:, :]], axis=0)
        conv = conv + convw_ref[SSD_CONV - 1 - j:SSD_CONV - j, :] * shifted
    xbc = _silu(conv)
    xs_ref[...] = xbc[:, :SSD_D_INNER]
    bc_ref[...] = xbc[:, SSD_D_INNER:].astype(BF16)

    cos = cos_ref[...]
    sin = sin_ref[...]
    k_pe = (p[:, _P_KPE:_P_KPE_ROT] * cos + p[:, _P_KPE_ROT:] * sin).astype(BF16)

    cqn = _rms_norm(p[:, _P_CQ:_P_CKV], qnw_ref[...]).astype(BF16)
    q = _dot(cqn, wq_ref[...])
    scale = (MLA_NOPE + MLA_ROPE) ** -0.5 * LOG2E
    hw = MLA_HEADS * LANES
    for hh in range(MLA_HEADS):
        lo = hh * LANES
        q_nope = q[:, lo:lo + LANES]
        q_pe = q[:, hw + lo:hw + lo + LANES] * cos + q[:, 2 * hw + lo:2 * hw + lo + LANES] * sin
        q_ref[:, hh * QK_SLAB:hh * QK_SLAB + MLA_NOPE] = (q_nope * scale).astype(BF16)
        q_ref[:, hh * QK_SLAB + MLA_NOPE:(hh + 1) * QK_SLAB] = (q_pe * scale).astype(BF16)

    ckvn = _rms_norm(p[:, _P_CKV:_P_KPE], kvnw_ref[...]).astype(BF16)
    kv = _dot(ckvn, wkv_ref[...])
    for hh in range(MLA_HEADS):
        k_ref[:, hh * QK_SLAB:hh * QK_SLAB + MLA_NOPE] = kv[:, hh * LANES:(hh + 1) * LANES].astype(BF16)
        k_ref[:, hh * QK_SLAB + MLA_NOPE:(hh + 1) * QK_SLAB] = k_pe
    v_ref[...] = kv[:, hw:].astype(BF16)


def _even_proj(x2, mod3, wc, conv_w, conv_b, qnw, wq, kvnw, wkv, cos_t, sin_t, *, seq, tm):
    t = x2.shape[0]
    tiles_per_seq = seq // tm
    row = lambda i: (i, 0)
    pos = lambda i: (i % tiles_per_seq, 0)
    out_shapes = (jax.ShapeDtypeStruct((t, SSD_D_INNER), F32),
                  jax.ShapeDtypeStruct((t, SSD_D_INNER), F32),
                  jax.ShapeDtypeStruct((t, 2 * SSD_GROUPS * SSD_STATE), BF16),
                  jax.ShapeDtypeStruct((t, LANES), F32),
                  jax.ShapeDtypeStruct((t, MLA_HEADS * QK_SLAB), BF16),
                  jax.ShapeDtypeStruct((t, MLA_HEADS * QK_SLAB), BF16),
                  jax.ShapeDtypeStruct((t, MLA_HEADS * MLA_V), BF16))
    return pl.pallas_call(
        functools.partial(_even_proj_kernel, tm=tm, tiles_per_seq=tiles_per_seq),
        grid=(t // tm,),
        in_specs=[pl.BlockSpec((tm, D_MODEL), row),
                  pl.BlockSpec((1, N_MOD, D_MODEL), lambda i: (i // tiles_per_seq, 0, 0)),
                  _resident(wc.shape), _resident(conv_w.shape), _resident(conv_b.shape),
                  _resident(qnw.shape), _resident(wq.shape),
                  _resident(kvnw.shape), _resident(wkv.shape),
                  pl.BlockSpec((tm, LANES), pos), pl.BlockSpec((tm, LANES), pos)],
        out_specs=tuple(pl.BlockSpec((tm, s.shape[1]), row) for s in out_shapes),
        out_shape=out_shapes,
        scratch_shapes=[pltpu.VMEM((SUBLANES, SSD_CONV_DIM), F32)],
        compiler_params=_params(1),
        name="even_proj",
    )(x2, mod3, wc, conv_w, conv_b, qnw, wq, kvnw, wkv, cos_t, sin_t)


_DONE = object()


def _ssd_chunk(xs, bm_b, cm_b, dt_raw, z, dtb, alog, dskip, nw, state_ref, out_ref, seq_idx):
    q = SSD_CHUNK
    lane = lax.broadcasted_iota(jnp.int32, (1, LANES), 1)
    head_lane = lane < SSD_HEADS
    dt = jax.nn.softplus(dt_raw + dtb)
    a = jnp.where(head_lane, -jnp.exp(alog), 0.0)
    adt = dt * a
    row = lax.broadcasted_iota(jnp.int32, (q, q), 0)
    col = lax.broadcasted_iota(jnp.int32, (q, q), 1)
    tril = row >= col
    tril_b = jnp.where(tril, 1.0, 0.0).astype(BF16)
    cs = _dot_split(tril_b, adt, 3)
    yield
    cs_last = cs[q - 1:q, :]
    ecs = jnp.exp(cs)
    dte = jnp.exp(cs_last - cs)
    cs2 = cs * LOG2E
    cs2_t = cs2.T

    er = lax.broadcasted_iota(jnp.int32, (LANES, SSD_D_INNER), 0)
    ec = lax.broadcasted_iota(jnp.int32, (LANES, SSD_D_INNER), 1)
    expand = jnp.where(ec // SSD_HEAD_DIM == er, 1.0, 0.0).astype(BF16)
    dt_x = _dot(dt.astype(BF16), expand)
    ecs_x = _dot_split_rhs(ecs, expand, 2)
    dte_x = _dot(dte.astype(BF16), expand)
    yield

    xc = xs * dt_x
    xc_b = xc.astype(BF16)
    xd_b = (xc * dte_x).astype(BF16)
    first_head = col < SSD_HEAD_DIM
    pairs_per_group = SSD_HEADS // SSD_GROUPS // 2
    ys = []
    for g in range(SSD_GROUPS):
        bg = bm_b[:, g * SSD_STATE:(g + 1) * SSD_STATE]
        cg = cm_b[:, g * SSD_STATE:(g + 1) * SSD_STATE]
        cb = _dot_nt(cg, bg)
        for r in range(pairs_per_group):
            j = g * pairs_per_group + r
            ps = slice(j * LANES, (j + 1) * LANES)
            ws = []
            for hd in (2 * j, 2 * j + 1):
                diff = cs2[:, hd:hd + 1] - cs2_t[hd:hd + 1, :]
                ws.append((cb * jnp.where(tril, jnp.exp2(diff), 0.0)).astype(BF16))
            xp = xc_b[:, ps]
            zero = jnp.zeros_like(xp)
            y_diag = _dot(jnp.concatenate(ws, axis=1),
                          jnp.concatenate([jnp.where(first_head, xp, zero),
                                           jnp.where(first_head, zero, xp)], axis=0))
            st = state_ref[seq_idx, j]
            ys.append(y_diag + ecs_x[:, ps] * _dot(cg, st.astype(BF16)))
            state_ref[seq_idx, j] = ecs_x[q - 1:q, ps] * st + _dot_tn(bg, xd_b[:, ps])
            yield

    y = jnp.concatenate(ys, axis=1) + dskip * xs
    y = y * _silu(z)
    gw = SSD_D_INNER // SSD_GROUPS
    out_ref[seq_idx] = jnp.concatenate(
        [_rms_norm(y[:, g * gw:(g + 1) * gw], nw[:, g * gw:(g + 1) * gw]) for g in range(SSD_GROUPS)],
        axis=1).astype(BF16)


def _ssd_kernel(xs_ref, bc_ref, dt_ref, z_ref, dtb_ref, alog_ref, dskip_ref, nw_ref, o_ref, state_ref):
    c = pl.program_id(1)

    @pl.when(c == 0)
    def _():
        state_ref[...] = jnp.zeros_like(state_ref)

    gn = SSD_GROUPS * SSD_STATE
    gens = [_ssd_chunk(xs_ref[s], bc_ref[s, :, :gn], bc_ref[s, :, gn:], dt_ref[s], z_ref[s],
                       dtb_ref[...], alog_ref[...], dskip_ref[...], nw_ref[...], state_ref, o_ref, s)
            for s in range(xs_ref.shape[0])]
    while gens:
        gens = [g for g in gens if next(g, _DONE) is not _DONE]


def _ssd(xs, bc, dt, z, dt_bias, a_log, d_skip_x, norm_w, *, batch, seq, seqs_per_step):
    q = SSD_CHUNK
    nc = seq // q
    g = seqs_per_step
    blk = lambda width: pl.BlockSpec((g, q, width), lambda b, c: (b, c, 0))
    as3 = lambda arr: arr.reshape(batch, seq, arr.shape[-1])
    out = pl.pallas_call(
        _ssd_kernel,
        grid=(batch // g, nc),
        in_specs=[blk(SSD_D_INNER), blk(2 * SSD_GROUPS * SSD_STATE), blk(LANES), blk(SSD_D_INNER),
                  _resident(dt_bias.shape), _resident(a_log.shape), _resident(d_skip_x.shape),
                  _resident(norm_w.shape)],
        out_specs=blk(SSD_D_INNER),
        out_shape=jax.ShapeDtypeStruct((batch, seq, SSD_D_INNER), BF16),
        scratch_shapes=[pltpu.VMEM((g, SSD_HEADS // 2, SSD_STATE, 2 * SSD_HEAD_DIM), F32)],
        compiler_params=_params(2),
        name="ssd_scan",
    )(as3(xs), as3(bc), as3(dt), as3(z), dt_bias, a_log, d_skip_x, norm_w)
    return out.reshape(batch * seq, SSD_D_INNER)


def _attn_kernel(q_ref, k_ref, v_ref, o_ref, *, tq):
    seq = q_ref.shape[0]
    row = lax.broadcasted_iota(jnp.int32, (tq, tq), 0)
    col = lax.broadcasted_iota(jnp.int32, (tq, tq), 1)
    causal = row >= col

    def scores(i):
        off = i * tq
        qt = q_ref[off:off + tq, :]
        s_d = jnp.where(causal, _dot_nt(qt, k_ref[off:off + tq, :]), -jnp.inf)
        s_o = _dot_nt(qt, k_ref[0:off, :]) if off else None
        return s_d, s_o

    n = seq // tq
    nxt = scores(0)
    for i in range(n):
        off = i * tq
        s_d, s_o = nxt
        if i + 1 < n:
            nxt = scores(i + 1)
        m = jnp.max(s_d, axis=-1, keepdims=True)
        if off:
            m = jnp.maximum(m, jnp.max(s_o, axis=-1, keepdims=True))
        p_d = jnp.exp2(s_d - m)
        l = jnp.sum(p_d, axis=-1, keepdims=True)
        acc = _dot(p_d.astype(BF16), v_ref[off:off + tq, :])
        if off:
            p_o = jnp.exp2(s_o - m)
            l = l + jnp.sum(p_o, axis=-1, keepdims=True)
            acc = acc + _dot(p_o.astype(BF16), v_ref[0:off, :])
        o_ref[off:off + tq, :] = (acc / l).astype(BF16)


def _attention(q, k, v, *, batch, seq, tq):
    t = batch * seq
    return pl.pallas_call(
        functools.partial(_attn_kernel, tq=tq),
        grid=(batch, MLA_HEADS),
        in_specs=[pl.BlockSpec((seq, QK_SLAB), lambda b, h: (b, h)),
                  pl.BlockSpec((seq, QK_SLAB), lambda b, h: (b, h)),
                  pl.BlockSpec((seq, MLA_V), lambda b, h: (b, h))],
        out_specs=pl.BlockSpec((seq, MLA_V), lambda b, h: (b, h)),
        out_shape=jax.ShapeDtypeStruct((t, MLA_HEADS * MLA_V), BF16),
        compiler_params=_params(2),
        name="mla_attention",
    )(q, k, v)


def _even_out_kernel(x_ref, mod_ref, ys_ref, ya_ref, wo_ref, g_ref, b_ref, o_ref):
    x = x_ref[...]
    _, _, gate = _mod_rows(mod_ref, 1)
    m = _dot(ys_ref[...], wo_ref[0:SSD_D_INNER, :]) + _dot(ya_ref[...], wo_ref[SSD_D_INNER:, :])
    y = DEEPNORM_ALPHA * x + (1.0 + gate) * m
    o_ref[...] = _layer_norm(y, g_ref[1:2, :], b_ref[1:2, :])


def _even_out(x2, mod3, y_ssd, y_att, w_out, ln_g, ln_b, *, seq, tm):
    t = x2.shape[0]
    tiles_per_seq = seq // tm
    row = lambda i: (i, 0)
    return pl.pallas_call(
        _even_out_kernel,
        grid=(t // tm,),
        in_specs=[pl.BlockSpec((tm, D_MODEL), row),
                  pl.BlockSpec((1, N_MOD, D_MODEL), lambda i: (i // tiles_per_seq, 0, 0)),
                  pl.BlockSpec((tm, SSD_D_INNER), row),
                  pl.BlockSpec((tm, MLA_HEADS * MLA_V), row),
                  _resident(w_out.shape), _resident(ln_g.shape), _resident(ln_b.shape)],
        out_specs=pl.BlockSpec((tm, D_MODEL), row),
        out_shape=jax.ShapeDtypeStruct((t, D_MODEL), F32),
        compiler_params=_params(1),
        name="even_out",
    )(x2, mod3, y_ssd, y_att, w_out, ln_g, ln_b)


def _sgu_kernel(x_ref, mod_ref, wuv_ref, buv_ref, sg_ref, sb_ref, ws_ref, bs_ref, wo_ref, g_ref, b_ref,
                *rest, tm):
    n_cast = (len(rest) - 2) // 2
    o_ref = rest[n_cast]
    gated_ref = rest[-1]
    _cast_blocks(rest[:n_cast], rest[n_cast + 1:-1])
    x = x_ref[...]
    sh, sc, gate = _mod_rows(mod_ref, 1)
    h = (x * (1.0 + sc) + sh).astype(BF16)
    gelu = lambda t: 0.5 * t * (1.0 + lax.erf(t * (2.0 ** -0.5)))
    v = gelu(_dot(h, wuv_ref[:, SGU_WIDTH:]) + buv_ref[:, SGU_WIDTH:])
    v = _layer_norm(v, sg_ref[...], sb_ref[...]).astype(BF16)
    u = gelu(_dot(h, wuv_ref[:, :SGU_WIDTH]) + buv_ref[:, :SGU_WIDTH])
    row = lax.broadcasted_iota(jnp.int32, (SGU_CHUNK, SGU_CHUNK), 0)
    col = lax.broadcasted_iota(jnp.int32, (SGU_CHUNK, SGU_CHUNK), 1)
    tril = row >= col
    for g in range(SGU_GROUPS):
        w = jnp.where(tril, ws_ref[g], 0.0).astype(BF16)
        gs = slice(g * SGU_GROUP_DIM, (g + 1) * SGU_GROUP_DIM)
        for cidx in range(tm // SGU_CHUNK):
            rs = slice(cidx * SGU_CHUNK, (cidx + 1) * SGU_CHUNK)
            sp = _dot(w, v[rs, gs]) + bs_ref[:, gs]
            gated_ref[rs, gs] = (u[rs, gs] * sp).astype(BF16)
    m = _dot(gated_ref[...], wo_ref[...])
    y = DEEPNORM_ALPHA * x + (1.0 + gate) * m
    o_ref[...] = _layer_norm(y, g_ref[1:2, :], b_ref[1:2, :])


def _sgu(x2, mod3, w_uv, b_uv, sgu_g, sgu_b, w_s, b_s_x, w_out, ln_g, ln_b, *, seq, tm, cast=()):
    t = x2.shape[0]
    steps = t // tm
    tiles_per_seq = seq // tm
    row = lambda i: (i, 0)
    views = [_cast_view(w, steps) for w in cast]
    cast_specs = _cast_specs(views, steps)
    outs = pl.pallas_call(
        functools.partial(_sgu_kernel, tm=tm),
        grid=(steps,),
        in_specs=[pl.BlockSpec((tm, D_MODEL), row),
                  pl.BlockSpec((1, N_MOD, D_MODEL), lambda i: (i // tiles_per_seq, 0, 0)),
                  _resident(w_uv.shape), _resident(b_uv.shape), _resident(sgu_g.shape),
                  _resident(sgu_b.shape), _resident(w_s.shape), _resident(b_s_x.shape),
                  _resident(w_out.shape), _resident(ln_g.shape), _resident(ln_b.shape)] + cast_specs,
        out_specs=[pl.BlockSpec((tm, D_MODEL), row)] + cast_specs,
        out_shape=[jax.ShapeDtypeStruct((t, D_MODEL), F32)]
        + [jax.ShapeDtypeStruct(v.shape, BF16) for v in views],
        scratch_shapes=[pltpu.VMEM((tm, SGU_WIDTH), BF16)],
        compiler_params=_params(1),
        name="sgu_mixer",
    )(x2, mod3, w_uv, b_uv, sgu_g, sgu_b, w_s, b_s_x, w_out, ln_g, ln_b, *views)
    return outs[0], [o.reshape(w.shape) for o, w in zip(outs[1:], cast)]


def _pad_cols(w, n):
    return jnp.pad(w, ((0, 0), (0, n - w.shape[1])))


def _rope_rot_cols(w):
    half = MLA_ROPE // 2
    return jnp.concatenate([-w[..., half:], w[..., :half]], axis=-1)


def _even_proj_weight(w_in):
    z, xbc, dt, cq, ckv, kr = jnp.split(
        w_in, [_OFF_XBC, _OFF_XBC + SSD_CONV_DIM, _OFF_XBC + SSD_CONV_DIM + SSD_HEADS,
               _OFF_XBC + SSD_CONV_DIM + SSD_HEADS + MLA_Q_RANK,
               _OFF_XBC + SSD_CONV_DIM + SSD_HEADS + MLA_Q_RANK + MLA_KV_RANK], axis=1)
    return jnp.concatenate([z, xbc, _pad_cols(dt, LANES), cq, ckv, _pad_cols(kr, LANES),
                            _pad_cols(_rope_rot_cols(kr), LANES)], axis=1).astype(BF16)


def _q_weight(w_uq):
    w = w_uq.reshape(MLA_Q_RANK, MLA_HEADS, MLA_NOPE + MLA_ROPE)
    nope = w[:, :, :MLA_NOPE].reshape(MLA_Q_RANK, -1)
    pe = w[:, :, MLA_NOPE:]
    pad = ((0, 0), (0, 0), (0, LANES - MLA_ROPE))
    pe_p = jnp.pad(pe, pad).reshape(MLA_Q_RANK, -1)
    rot_p = jnp.pad(_rope_rot_cols(pe), pad).reshape(MLA_Q_RANK, -1)
    return jnp.concatenate([nope, pe_p, rot_p], axis=1).astype(BF16)


def _kv_weight(w_ukv):
    w = w_ukv.reshape(MLA_KV_RANK, MLA_HEADS, MLA_NOPE + MLA_V)
    return jnp.concatenate([w[:, :, :MLA_NOPE].reshape(MLA_KV_RANK, -1),
                            w[:, :, MLA_NOPE:].reshape(MLA_KV_RANK, -1)], axis=1).astype(BF16)


def _rope_tables(seq):
    inv = 1.0 / (ROPE_THETA ** (jnp.arange(0, MLA_ROPE, 2, dtype=F32) / MLA_ROPE))
    ang = jnp.arange(seq, dtype=F32)[:, None] * inv[None, :]
    zeros = jnp.zeros((seq, LANES - MLA_ROPE), F32)
    cos, sin = jnp.cos(ang), jnp.sin(ang)
    return (jnp.concatenate([cos, cos, zeros], axis=1), jnp.concatenate([sin, sin, zeros], axis=1))


def _pad_lanes_row(v):
    return jnp.pad(v, (0, LANES - v.shape[0])).reshape(1, LANES)


def kernel(x, c, l0_ada_w, l0_ada_b, l0_ln_g, l0_ln_b, l0_ffa_w_in, l0_ffa_w_out, l0_ffb_w_in, l0_ffb_w_out, l0_w_in, l0_conv_w, l0_conv_b, l0_dt_bias, l0_a_log, l0_d_skip, l0_ssd_norm_w, l0_q_norm_w, l0_w_uq, l0_kv_norm_w, l0_w_ukv, l0_w_out, l1_ada_w, l1_ada_b, l1_ln_g, l1_ln_b, l1_ffa_w_in, l1_ffa_w_out, l1_ffb_w_in, l1_ffb_w_out, l1_w_uv, l1_b_uv, l1_sgu_ln_g, l1_sgu_ln_b, l1_w_s, l1_b_s, l1_w_out):
    batch, seq, d = x.shape
    assert d == D_MODEL and seq % SSD_CHUNK == 0
    tm = math.gcd(seq, 512)
    tq = math.gcd(seq, 256)
    x2 = x.reshape(batch * seq, d)
    bf = lambda w: w.astype(BF16)

    mod = _ada(c, l0_ada_w, l0_ada_b).reshape(batch, N_MOD, d)
    x2, (ffb_in, ffb_out, mix_out) = _ffn(
        x2, mod, bf(l0_ffa_w_in), bf(l0_ffa_w_out), l0_ln_g, l0_ln_b, sub=0, seq=seq, tm=tm,
        cast=(l0_ffb_w_in, l0_ffb_w_out, l0_w_out))
    cos_t, sin_t = _rope_tables(seq)
    z, xs, bc, dt, q, k, v = _even_proj(
        x2, mod, _even_proj_weight(l0_w_in), l0_conv_w, l0_conv_b.reshape(1, -1),
        l0_q_norm_w.reshape(1, -1), _q_weight(l0_w_uq),
        l0_kv_norm_w.reshape(1, -1), _kv_weight(l0_w_ukv), cos_t, sin_t, seq=seq, tm=tm)
    y_ssd = _ssd(xs, bc, dt, z, _pad_lanes_row(l0_dt_bias),
                 _pad_lanes_row(l0_a_log), jnp.repeat(l0_d_skip, SSD_HEAD_DIM).reshape(1, -1),
                 l0_ssd_norm_w.reshape(1, -1), batch=batch, seq=seq,
                 seqs_per_step=math.gcd(batch, 4))
    y_att = _attention(q, k, v, batch=batch, seq=seq, tq=tq)
    x2 = _even_out(x2, mod, y_ssd, y_att, mix_out, l0_ln_g, l0_ln_b, seq=seq, tm=tm)
    x2, (ffa_in, ffa_out) = _ffn(x2, mod, ffb_in, ffb_out, l0_ln_g, l0_ln_b, sub=2, seq=seq, tm=tm,
                                 cast=(l1_ffa_w_in, l1_ffa_w_out))

    mod = _ada(c, l1_ada_w, l1_ada_b).reshape(batch, N_MOD, d)
    x2, (w_uv, mix_out) = _ffn(x2, mod, ffa_in, ffa_out, l1_ln_g, l1_ln_b, sub=0, seq=seq, tm=tm,
                               cast=(l1_w_uv, l1_w_out))
    b_s_x = jnp.repeat(l1_b_s.T, SGU_GROUP_DIM, axis=1)
    x2, (ffb_in, ffb_out) = _sgu(
        x2, mod, w_uv, l1_b_uv.reshape(1, -1), l1_sgu_ln_g.reshape(1, -1),
        l1_sgu_ln_b.reshape(1, -1), l1_w_s, b_s_x, mix_out, l1_ln_g, l1_ln_b,
        seq=seq, tm=tm, cast=(l1_ffb_w_in, l1_ffb_w_out))
    x2, _ = _ffn(x2, mod, ffb_in, ffb_out, l1_ln_g, l1_ln_b, sub=2, seq=seq, tm=tm)
    return x2.reshape(batch, seq, d)
```

```python
import functools
import math

import jax
import jax.numpy as jnp
from jax import lax
from jax.experimental import pallas as pl
from jax.experimental.pallas import tpu as pltpu

F32 = jnp.float32
BF16 = jnp.bfloat16

D_MODEL = 1024
DEPTH = 2
D_FF = 2816
SSD_HEADS = 16
SSD_HEAD_DIM = 64
SSD_D_INNER = SSD_HEADS * SSD_HEAD_DIM
SSD_GROUPS = 2
SSD_STATE = 128
SSD_CONV = 4
SSD_CHUNK = 128
SSD_CONV_DIM = SSD_D_INNER + 2 * SSD_GROUPS * SSD_STATE
MLA_HEADS = 8
MLA_Q_RANK = 384
MLA_KV_RANK = 256
MLA_NOPE = 128
MLA_ROPE = 64
MLA_V = 128
ROPE_THETA = 10000.0
SGU_WIDTH = 2 * D_MODEL
SGU_GROUPS = 16
SGU_GROUP_DIM = SGU_WIDTH // SGU_GROUPS
SGU_CHUNK = 128
DEEPNORM_ALPHA = (2 * DEPTH) ** 0.25
N_MOD = 9
EPS = 1e-5
LOG2E = math.log2(math.e)

LANES = 128
SUBLANES = 8
VMEM_BYTES_V7X = 64 * 1024 * 1024
VMEM_LIMIT = (VMEM_BYTES_V7X * 3) // 4

QK_SLAB = MLA_NOPE + LANES

_OFF_Z = 0
_OFF_XBC = _OFF_Z + SSD_D_INNER
_OFF_DT = _OFF_XBC + SSD_CONV_DIM
_OFF_CQ = _OFF_DT + LANES
_OFF_CKV = _OFF_CQ + MLA_Q_RANK
_OFF_KPE = _OFF_CKV + MLA_KV_RANK
_OFF_KPE_ROT = _OFF_KPE + LANES
_PROJ_COLS = _OFF_KPE_ROT + LANES
_P_CQ = _OFF_CQ - _OFF_DT
_P_CKV = _OFF_CKV - _OFF_DT
_P_KPE = _OFF_KPE - _OFF_DT
_P_KPE_ROT = _OFF_KPE_ROT - _OFF_DT


def _silu(x):
    return x * jax.nn.sigmoid(x)


def _layer_norm(y, g, b):
    mu = jnp.mean(y, axis=-1, keepdims=True)
    yc = y - mu
    var = jnp.mean(yc * yc, axis=-1, keepdims=True)
    return yc * lax.rsqrt(var + EPS) * g + b


def _rms_norm(y, w):
    return y * lax.rsqrt(jnp.mean(y * y, axis=-1, keepdims=True) + EPS) * w


def _dot(a, b):
    return jnp.dot(a, b, preferred_element_type=F32)


def _dot_nt(a, b):
    return lax.dot_general(a, b, (((1,), (1,)), ((), ())), preferred_element_type=F32)


def _dot_tn(a, b):
    return lax.dot_general(a, b, (((0,), (0,)), ((), ())), preferred_element_type=F32)


def _split_bf16(x, terms):
    parts = []
    r = x
    for _ in range(terms):
        p = r.astype(BF16)
        parts.append(p)
        r = r - p.astype(F32)
    return parts


def _dot_split(a_bf16, x, terms):
    acc = None
    for p in _split_bf16(x, terms):
        t = _dot(a_bf16, p)
        acc = t if acc is None else acc + t
    return acc


def _dot_split_rhs(x, b_bf16, terms):
    acc = None
    for p in _split_bf16(x, terms):
        t = _dot(p, b_bf16)
        acc = t if acc is None else acc + t
    return acc


def _resident(shape):
    n = len(shape)
    return pl.BlockSpec(shape, lambda *_: (0,) * n, pipeline_mode=pl.Buffered(1))


def _cast_spec(w, steps):
    rows = w.shape[0]
    unit = 2 * SUBLANES
    blk = next(r for r in range(unit, rows + 1, unit) if rows % r == 0 and rows // r <= steps)
    last = rows // blk - 1
    return pl.BlockSpec((blk, w.shape[1]), lambda i: (jnp.minimum(i, last), 0))


def _cast_blocks(src_refs, dst_refs):
    for s_ref, d_ref in zip(src_refs, dst_refs):
        d_ref[...] = s_ref[...].astype(BF16)


def _params(n_axes):
    return pltpu.CompilerParams(dimension_semantics=("arbitrary",) * n_axes,
                                vmem_limit_bytes=VMEM_LIMIT)


def _ada_kernel(c_ref, w_ref, b_ref, *rest):
    n_cast = (len(rest) - 1) // 2
    o_ref = rest[n_cast]
    _cast_blocks(rest[:n_cast], rest[n_cast + 1:])
    s = _silu(c_ref[...]).astype(BF16)
    o_ref[...] = _dot(s, w_ref[...].astype(BF16)) + b_ref[...]


def _ada(c, ada_w, ada_b, cast=()):
    b, d = c.shape
    n = ada_w.shape[1]
    tn = D_MODEL
    steps = n // tn
    cast_specs = [_cast_spec(w, steps) for w in cast]
    outs = pl.pallas_call(
        _ada_kernel,
        grid=(steps,),
        in_specs=[pl.BlockSpec((b, d), lambda j: (0, 0)),
                  pl.BlockSpec((d, tn), lambda j: (0, j)),
                  pl.BlockSpec((1, tn), lambda j: (0, j))] + cast_specs,
        out_specs=[pl.BlockSpec((b, tn), lambda j: (0, j))] + cast_specs,
        out_shape=[jax.ShapeDtypeStruct((b, n), F32)]
        + [jax.ShapeDtypeStruct(w.shape, BF16) for w in cast],
        compiler_params=_params(1),
        name="ada_mod",
    )(c, ada_w, ada_b.reshape(1, n), *cast)
    return outs[0], outs[1:]


def _mod_rows(mod_ref, k):
    return (mod_ref[0, 3 * k:3 * k + 1, :], mod_ref[0, 3 * k + 1:3 * k + 2, :],
            mod_ref[0, 3 * k + 2:3 * k + 3, :])


def _ffn_kernel(x_ref, mod_ref, win_ref, wout_ref, g_ref, b_ref, *rest, sub):
    n_cast = (len(rest) - 1) // 2
    o_ref = rest[n_cast]
    _cast_blocks(rest[:n_cast], rest[n_cast + 1:])
    x = x_ref[...]
    sh, sc, gate = _mod_rows(mod_ref, sub)
    h = (x * (1.0 + sc) + sh).astype(BF16)
    ab = _dot(h, win_ref[...])
    a = ab[:, :D_FF]
    b = ab[:, D_FF:]
    act = (_silu(a) * b).astype(BF16)
    f = _dot(act, wout_ref[...])
    y = DEEPNORM_ALPHA * x + (1.0 + gate) * (0.5 * f)
    o_ref[...] = _layer_norm(y, g_ref[sub:sub + 1, :], b_ref[sub:sub + 1, :])


def _ffn(x2, mod3, w_in, w_out, ln_g, ln_b, *, sub, seq, tm, cast=()):
    t = x2.shape[0]
    steps = t // tm
    tiles_per_seq = seq // tm
    cast_specs = [_cast_spec(w, steps) for w in cast]
    outs = pl.pallas_call(
        functools.partial(_ffn_kernel, sub=sub),
        grid=(steps,),
        in_specs=[pl.BlockSpec((tm, D_MODEL), lambda i: (i, 0)),
                  pl.BlockSpec((1, N_MOD, D_MODEL), lambda i: (i // tiles_per_seq, 0, 0)),
                  _resident(w_in.shape), _resident(w_out.shape),
                  _resident(ln_g.shape), _resident(ln_b.shape)] + cast_specs,
        out_specs=[pl.BlockSpec((tm, D_MODEL), lambda i: (i, 0))] + cast_specs,
        out_shape=[jax.ShapeDtypeStruct((t, D_MODEL), F32)]
        + [jax.ShapeDtypeStruct(w.shape, BF16) for w in cast],
        compiler_params=_params(1),
        name="ffn_sublayer",
    )(x2, mod3, w_in, w_out, ln_g, ln_b, *cast)
    return outs[0], outs[1:]


def _even_proj_kernel(x_ref, mod_ref, wc_ref, convw_ref, convb_ref, qnw_ref, wq_ref, kvnw_ref, wkv_ref,
                      cos_ref, sin_ref, z_ref, xs_ref, bc_ref, dt_ref, q_ref, k_ref, v_ref, tail_ref,
                      *, tm, tiles_per_seq):
    i = pl.program_id(0)

    @pl.when(i == 0)
    def _():
        tail_ref[...] = jnp.zeros((SUBLANES, SSD_CONV_DIM), F32)

    x = x_ref[...]
    sh, sc, _ = _mod_rows(mod_ref, 1)
    h = (x * (1.0 + sc) + sh).astype(BF16)
    seq_start = (i % tiles_per_seq) == 0
    tail = jnp.where(seq_start, 0.0, tail_ref[...])
    xr = _dot(h, wc_ref[:, _OFF_XBC:_OFF_DT])
    tail_ref[...] = xr[tm - SUBLANES:, :]
    z_ref[...] = _dot(h, wc_ref[:, _OFF_Z:_OFF_XBC])
    p = _dot(h, wc_ref[:, _OFF_DT:])
    dt_ref[...] = p[:, 0:LANES]
    sub_id = lax.broadcasted_iota(jnp.int32, (SUBLANES, SSD_CONV_DIM), 0)
    conv = convb_ref[...] + convw_ref[SSD_CONV - 1:SSD_CONV, :] * xr
    for j in range(1, SSD_CONV):
        rolled = pltpu.roll(xr, j, axis=0)
        head = jnp.where(sub_id < j, pltpu.roll(tail, j, axis=0), rolled[0:SUBLANES, :])
        shifted = jnp.concatenate([head, rolled[SUBLANES:, :]], axis=0)
        conv = conv + convw_ref[SSD_CONV - 1 - j:SSD_CONV - j, :] * shifted
    xbc = _silu(conv)
    xs_ref[...] = xbc[:, :SSD_D_INNER]
    bc_ref[...] = xbc[:, SSD_D_INNER:].astype(BF16)

    cos = cos_ref[...]
    sin = sin_ref[...]
    k_pe = (p[:, _P_KPE:_P_KPE_ROT] * cos + p[:, _P_KPE_ROT:] * sin).astype(BF16)

    cqn = _rms_norm(p[:, _P_CQ:_P_CKV], qnw_ref[...]).astype(BF16)
    q = _dot(cqn, wq_ref[...])
    scale = (MLA_NOPE + MLA_ROPE) ** -0.5 * LOG2E
    hw = MLA_HEADS * LANES
    for hh in range(MLA_HEADS):
        lo = hh * LANES
        q_nope = q[:, lo:lo + LANES]
        q_pe = q[:, hw + lo:hw + lo + LANES] * cos + q[:, 2 * hw + lo:2 * hw + lo + LANES] * sin
        q_ref[:, hh * QK_SLAB:hh * QK_SLAB + MLA_NOPE] = (q_nope * scale).astype(BF16)
        q_ref[:, hh * QK_SLAB + MLA_NOPE:(hh + 1) * QK_SLAB] = (q_pe * scale).astype(BF16)

    ckvn = _rms_norm(p[:, _P_CKV:_P_KPE], kvnw_ref[...]).astype(BF16)
    kv = _dot(ckvn, wkv_ref[...])
    for hh in range(MLA_HEADS):
        k_ref[:, hh * QK_SLAB:hh * QK_SLAB + MLA_NOPE] = kv[:, hh * LANES:(hh + 1) * LANES].astype(BF16)
        k_ref[:, hh * QK_SLAB + MLA_NOPE:(hh + 1) * QK_SLAB] = k_pe
    v_ref[...] = kv[:, hw:].astype(BF16)


def _even_proj(x2, mod3, wc, conv_w, conv_b, qnw, wq, kvnw, wkv, cos_t, sin_t, *, seq, tm):
    t = x2.shape[0]
    tiles_per_seq = seq // tm
    row = lambda i: (i, 0)
    pos = lambda i: (i % tiles_per_seq, 0)
    out_shapes = (jax.ShapeDtypeStruct((t, SSD_D_INNER), F32),
                  jax.ShapeDtypeStruct((t, SSD_D_INNER), F32),
                  jax.ShapeDtypeStruct((t, 2 * SSD_GROUPS * SSD_STATE), BF16),
                  jax.ShapeDtypeStruct((t, LANES), F32),
                  jax.ShapeDtypeStruct((t, MLA_HEADS * QK_SLAB), BF16),
                  jax.ShapeDtypeStruct((t, MLA_HEADS * QK_SLAB), BF16),
                  jax.ShapeDtypeStruct((t, MLA_HEADS * MLA_V), BF16))
    return pl.pallas_call(
        functools.partial(_even_proj_kernel, tm=tm, tiles_per_seq=tiles_per_seq),
        grid=(t // tm,),
        in_specs=[pl.BlockSpec((tm, D_MODEL), row),
                  pl.BlockSpec((1, N_MOD, D_MODEL), lambda i: (i // tiles_per_seq, 0, 0)),
                  _resident(wc.shape), _resident(conv_w.shape), _resident(conv_b.shape),
                  _resident(qnw.shape), _resident(wq.shape),
                  _resident(kvnw.shape), _resident(wkv.shape),
                  pl.BlockSpec((tm, LANES), pos), pl.BlockSpec((tm, LANES), pos)],
        out_specs=tuple(pl.BlockSpec((tm, s.shape[1]), row) for s in out_shapes),
        out_shape=out_shapes,
        scratch_shapes=[pltpu.VMEM((SUBLANES, SSD_CONV_DIM), F32)],
        compiler_params=_params(1),
        name="even_proj",
    )(x2, mod3, wc, conv_w, conv_b, qnw, wq, kvnw, wkv, cos_t, sin_t)


_DONE = object()


def _ssd_chunk(xs, bm_b, cm_b, dt_raw, z, dtb, alog, dskip, nw, state_ref, out_ref, seq_idx):
    q = SSD_CHUNK
    lane = lax.broadcasted_iota(jnp.int32, (1, LANES), 1)
    head_lane = lane < SSD_HEADS
    dt = jax.nn.softplus(dt_raw + dtb)
    a = jnp.where(head_lane, -jnp.exp(alog), 0.0)
    adt = dt * a
    row = lax.broadcasted_iota(jnp.int32, (q, q), 0)
    col = lax.broadcasted_iota(jnp.int32, (q, q), 1)
    tril = row >= col
    tril_b = jnp.where(tril, 1.0, 0.0).astype(BF16)
    cs = _dot_split(tril_b, adt, 3)
    yield
    cs_last = cs[q - 1:q, :]
    ecs = jnp.exp(cs)
    dte = jnp.exp(cs_last - cs)
    cs2 = cs * LOG2E
    cs2_t = cs2.T

    er = lax.broadcasted_iota(jnp.int32, (LANES, SSD_D_INNER), 0)
    ec = lax.broadcasted_iota(jnp.int32, (LANES, SSD_D_INNER), 1)
    expand = jnp.where(ec // SSD_HEAD_DIM == er, 1.0, 0.0).astype(BF16)
    dt_x = _dot(dt.astype(BF16), expand)
    ecs_x = _dot_split_rhs(ecs, expand, 2)
    dte_x = _dot(dte.astype(BF16), expand)
    yield

    xc = xs * dt_x
    xc_b = xc.astype(BF16)
    xd_b = (xc * dte_x).astype(BF16)
    first_head = col < SSD_HEAD_DIM
    pairs_per_group = SSD_HEADS // SSD_GROUPS // 2
    ys = []
    for g in range(SSD_GROUPS):
        bg = bm_b[:, g * SSD_STATE:(g + 1) * SSD_STATE]
        cg = cm_b[:, g * SSD_STATE:(g + 1) * SSD_STATE]
        cb = _dot_nt(cg, bg)
        for r in range(pairs_per_group):
            j = g * pairs_per_group + r
            ps = slice(j * LANES, (j + 1) * LANES)
            ws = []
            for hd in (2 * j, 2 * j + 1):
                diff = cs2[:, hd:hd + 1] - cs2_t[hd:hd + 1, :]
                ws.append((cb * jnp.where(tril, jnp.exp2(diff), 0.0)).astype(BF16))
            xp = xc_b[:, ps]
            zero = jnp.zeros_like(xp)
            y_diag = _dot(jnp.concatenate(ws, axis=1),
                          jnp.concatenate([jnp.where(first_head, xp, zero),
                                           jnp.where(first_head, zero, xp)], axis=0))
            st = state_ref[seq_idx, j]
            ys.append(y_diag + ecs_x[:, ps] * _dot(cg, st.astype(BF16)))
            state_ref[seq_idx, j] = ecs_x[q - 1:q, ps] * st + _dot_tn(bg, xd_b[:, ps])
            yield

    y = jnp.concatenate(ys, axis=1) + dskip * xs
    y = y * _silu(z)
    gw = SSD_D_INNER // SSD_GROUPS
    out_ref[seq_idx] = jnp.concatenate(
        [_rms_norm(y[:, g * gw:(g + 1) * gw], nw[:, g * gw:(g + 1) * gw]) for g in range(SSD_GROUPS)],
        axis=1).astype(BF16)


def _ssd_kernel(xs_ref, bc_ref, dt_ref, z_ref, dtb_ref, alog_ref, dskip_ref, nw_ref, o_ref, state_ref):
    c = pl.program_id(1)

    @pl.when(c == 0)
    def _():
        state_ref[...] = jnp.zeros_like(state_ref)

    gn = SSD_GROUPS * SSD_STATE
    gens = [_ssd_chunk(xs_ref[s], bc_ref[s, :, :gn], bc_ref[s, :, gn:], dt_ref[s], z_ref[s],
                       dtb_ref[...], alog_ref[...], dskip_ref[...], nw_ref[...], state_ref, o_ref, s)
            for s in range(xs_ref.shape[0])]
    while gens:
        gens = [g for g in gens if next(g, _DONE) is not _DONE]


def _ssd(xs, bc, dt, z, dt_bias, a_log, d_skip_x, norm_w, *, batch, seq, seqs_per_step):
    q = SSD_CHUNK
    nc = seq // q
    g = seqs_per_step
    blk = lambda width: pl.BlockSpec((g, q, width), lambda b, c: (b, c, 0))
    as3 = lambda arr: arr.reshape(batch, seq, arr.shape[-1])
    out = pl.pallas_call(
        _ssd_kernel,
        grid=(batch // g, nc),
        in_specs=[blk(SSD_D_INNER), blk(2 * SSD_GROUPS * SSD_STATE), blk(LANES), blk(SSD_D_INNER),
                  _resident(dt_bias.shape), _resident(a_log.shape), _resident(d_skip_x.shape),
                  _resident(norm_w.shape)],
        out_specs=blk(SSD_D_INNER),
        out_shape=jax.ShapeDtypeStruct((batch, seq, SSD_D_INNER), BF16),
        scratch_shapes=[pltpu.VMEM((g, SSD_HEADS // 2, SSD_STATE, 2 * SSD_HEAD_DIM), F32)],
        compiler_params=_params(2),
        name="ssd_scan",
    )(as3(xs), as3(bc), as3(dt), as3(z), dt_bias, a_log, d_skip_x, norm_w)
    return out.reshape(batch * seq, SSD_D_INNER)


def _attn_kernel(q_ref, k_ref, v_ref, o_ref, *, tq):
    seq = q_ref.shape[0]
    row = lax.broadcasted_iota(jnp.int32, (tq, tq), 0)
    col = lax.broadcasted_iota(jnp.int32, (tq, tq), 1)
    causal = row >= col

    def scores(i):
        off = i * tq
        qt = q_ref[off:off + tq, :]
        s_d = jnp.where(causal, _dot_nt(qt, k_ref[off:off + tq, :]), -jnp.inf)
        s_o = _dot_nt(qt, k_ref[0:off, :]) if off else None
        return s_d, s_o

    n = seq // tq
    nxt = scores(0)
    for i in range(n):
        off = i * tq
        s_d, s_o = nxt
        if i + 1 < n:
            nxt = scores(i + 1)
        m = jnp.max(s_d, axis=-1, keepdims=True)
        if off:
            m = jnp.maximum(m, jnp.max(s_o, axis=-1, keepdims=True))
        p_d = jnp.exp2(s_d - m)
        l = jnp.sum(p_d, axis=-1, keepdims=True)
        acc = _dot(p_d.astype(BF16), v_ref[off:off + tq, :])
        if off:
            p_o = jnp.exp2(s_o - m)
            l = l + jnp.sum(p_o, axis=-1, keepdims=True)
            acc = acc + _dot(p_o.astype(BF16), v_ref[0:off, :])
        o_ref[off:off + tq, :] = (acc / l).astype(BF16)


def _attention(q, k, v, *, batch, seq, tq):
    t = batch * seq
    return pl.pallas_call(
        functools.partial(_attn_kernel, tq=tq),
        grid=(batch, MLA_HEADS),
        in_specs=[pl.BlockSpec((seq, QK_SLAB), lambda b, h: (b, h)),
                  pl.BlockSpec((seq, QK_SLAB), lambda b, h: (b, h)),
                  pl.BlockSpec((seq, MLA_V), lambda b, h: (b, h))],
        out_specs=pl.BlockSpec((seq, MLA_V), lambda b, h: (b, h)),
        out_shape=jax.ShapeDtypeStruct((t, MLA_HEADS * MLA_V), BF16),
        compiler_params=_params(2),
        name="mla_attention",
    )(q, k, v)


def _even_out_kernel(x_ref, mod_ref, ys_ref, ya_ref, wo_ref, g_ref, b_ref, o_ref):
    x = x_ref[...]
    _, _, gate = _mod_rows(mod_ref, 1)
    m = _dot(ys_ref[...], wo_ref[0:SSD_D_INNER, :]) + _dot(ya_ref[...], wo_ref[SSD_D_INNER:, :])
    y = DEEPNORM_ALPHA * x + (1.0 + gate) * m
    o_ref[...] = _layer_norm(y, g_ref[1:2, :], b_ref[1:2, :])


def _even_out(x2, mod3, y_ssd, y_att, w_out, ln_g, ln_b, *, seq, tm):
    t = x2.shape[0]
    tiles_per_seq = seq // tm
    row = lambda i: (i, 0)
    return pl.pallas_call(
        _even_out_kernel,
        grid=(t // tm,),
        in_specs=[pl.BlockSpec((tm, D_MODEL), row),
                  pl.BlockSpec((1, N_MOD, D_MODEL), lambda i: (i // tiles_per_seq, 0, 0)),
                  pl.BlockSpec((tm, SSD_D_INNER), row),
                  pl.BlockSpec((tm, MLA_HEADS * MLA_V), row),
                  _resident(w_out.shape), _resident(ln_g.shape), _resident(ln_b.shape)],
        out_specs=pl.BlockSpec((tm, D_MODEL), row),
        out_shape=jax.ShapeDtypeStruct((t, D_MODEL), F32),
        compiler_params=_params(1),
        name="even_out",
    )(x2, mod3, y_ssd, y_att, w_out, ln_g, ln_b)


def _sgu_kernel(x_ref, mod_ref, wuv_ref, buv_ref, sg_ref, sb_ref, ws_ref, bs_ref, wo_ref, g_ref, b_ref,
                *rest, tm):
    n_cast = (len(rest) - 2) // 2
    o_ref = rest[n_cast]
    gated_ref = rest[-1]
    _cast_blocks(rest[:n_cast], rest[n_cast + 1:-1])
    x = x_ref[...]
    sh, sc, gate = _mod_rows(mod_ref, 1)
    h = (x * (1.0 + sc) + sh).astype(BF16)
    gelu = lambda t: 0.5 * t * (1.0 + lax.erf(t * (2.0 ** -0.5)))
    v = gelu(_dot(h, wuv_ref[:, SGU_WIDTH:]) + buv_ref[:, SGU_WIDTH:])
    v = _layer_norm(v, sg_ref[...], sb_ref[...]).astype(BF16)
    u = gelu(_dot(h, wuv_ref[:, :SGU_WIDTH]) + buv_ref[:, :SGU_WIDTH])
    row = lax.broadcasted_iota(jnp.int32, (SGU_CHUNK, SGU_CHUNK), 0)
    col = lax.broadcasted_iota(jnp.int32, (SGU_CHUNK, SGU_CHUNK), 1)
    tril = row >= col
    for g in range(SGU_GROUPS):
        w = jnp.where(tril, ws_ref[g], 0.0).astype(BF16)
        gs = slice(g * SGU_GROUP_DIM, (g + 1) * SGU_GROUP_DIM)
        for cidx in range(tm // SGU_CHUNK):
            rs = slice(cidx * SGU_CHUNK, (cidx + 1) * SGU_CHUNK)
            sp = _dot(w, v[rs, gs]) + bs_ref[:, gs]
            gated_ref[rs, gs] = (u[rs, gs] * sp).astype(BF16)
    m = _dot(gated_ref[...], wo_ref[...])
    y = DEEPNORM_ALPHA * x + (1.0 + gate) * m
    o_ref[...] = _layer_norm(y, g_ref[1:2, :], b_ref[1:2, :])


def _sgu(x2, mod3, w_uv, b_uv, sgu_g, sgu_b, w_s, b_s_x, w_out, ln_g, ln_b, *, seq, tm, cast=()):
    t = x2.shape[0]
    steps = t // tm
    tiles_per_seq = seq // tm
    row = lambda i: (i, 0)
    cast_specs = [_cast_spec(w, steps) for w in cast]
    outs = pl.pallas_call(
        functools.partial(_sgu_kernel, tm=tm),
        grid=(steps,),
        in_specs=[pl.BlockSpec((tm, D_MODEL), row),
                  pl.BlockSpec((1, N_MOD, D_MODEL), lambda i: (i // tiles_per_seq, 0, 0)),
                  _resident(w_uv.shape), _resident(b_uv.shape), _resident(sgu_g.shape),
                  _resident(sgu_b.shape), _resident(w_s.shape), _resident(b_s_x.shape),
                  _resident(w_out.shape), _resident(ln_g.shape), _resident(ln_b.shape)] + cast_specs,
        out_specs=[pl.BlockSpec((tm, D_MODEL), row)] + cast_specs,
        out_shape=[jax.ShapeDtypeStruct((t, D_MODEL), F32)]
        + [jax.ShapeDtypeStruct(w.shape, BF16) for w in cast],
        scratch_shapes=[pltpu.VMEM((tm, SGU_WIDTH), BF16)],
        compiler_params=_params(1),
        name="sgu_mixer",
    )(x2, mod3, w_uv, b_uv, sgu_g, sgu_b, w_s, b_s_x, w_out, ln_g, ln_b, *cast)
    return outs[0], outs[1:]


def _pad_cols(w, n):
    return jnp.pad(w, ((0, 0), (0, n - w.shape[1])))


def _rope_rot_cols(w):
    half = MLA_ROPE // 2
    return jnp.concatenate([-w[..., half:], w[..., :half]], axis=-1)


def _even_proj_weight(w_in):
    z, xbc, dt, cq, ckv, kr = jnp.split(
        w_in, [_OFF_XBC, _OFF_XBC + SSD_CONV_DIM, _OFF_XBC + SSD_CONV_DIM + SSD_HEADS,
               _OFF_XBC + SSD_CONV_DIM + SSD_HEADS + MLA_Q_RANK,
               _OFF_XBC + SSD_CONV_DIM + SSD_HEADS + MLA_Q_RANK + MLA_KV_RANK], axis=1)
    return jnp.concatenate([z, xbc, _pad_cols(dt, LANES), cq, ckv, _pad_cols(kr, LANES),
                            _pad_cols(_rope_rot_cols(kr), LANES)], axis=1).astype(BF16)


def _q_weight(w_uq):
    w = w_uq.reshape(MLA_Q_RANK, MLA_HEADS, MLA_NOPE + MLA_ROPE)
    nope = w[:, :, :MLA_NOPE].reshape(MLA_Q_RANK, -1)
    pe = w[:, :, MLA_NOPE:]
    pad = ((0, 0), (0, 0), (0, LANES - MLA_ROPE))
    pe_p = jnp.pad(pe, pad).reshape(MLA_Q_RANK, -1)
    rot_p = jnp.pad(_rope_rot_cols(pe), pad).reshape(MLA_Q_RANK, -1)
    return jnp.concatenate([nope, pe_p, rot_p], axis=1).astype(BF16)


def _kv_weight(w_ukv):
    w = w_ukv.reshape(MLA_KV_RANK, MLA_HEADS, MLA_NOPE + MLA_V)
    return jnp.concatenate([w[:, :, :MLA_NOPE].reshape(MLA_KV_RANK, -1),
                            w[:, :, MLA_NOPE:].reshape(MLA_KV_RANK, -1)], axis=1).astype(BF16)


def _rope_tables(seq):
    inv = 1.0 / (ROPE_THETA ** (jnp.arange(0, MLA_ROPE, 2, dtype=F32) / MLA_ROPE))
    ang = jnp.arange(seq, dtype=F32)[:, None] * inv[None, :]
    zeros = jnp.zeros((seq, LANES - MLA_ROPE), F32)
    cos, sin = jnp.cos(ang), jnp.sin(ang)
    return (jnp.concatenate([cos, cos, zeros], axis=1), jnp.concatenate([sin, sin, zeros], axis=1))


def _pad_lanes_row(v):
    return jnp.pad(v, (0, LANES - v.shape[0])).reshape(1, LANES)


def kernel(x, c, l0_ada_w, l0_ada_b, l0_ln_g, l0_ln_b, l0_ffa_w_in, l0_ffa_w_out, l0_ffb_w_in, l0_ffb_w_out, l0_w_in, l0_conv_w, l0_conv_b, l0_dt_bias, l0_a_log, l0_d_skip, l0_ssd_norm_w, l0_q_norm_w, l0_w_uq, l0_kv_norm_w, l0_w_ukv, l0_w_out, l1_ada_w, l1_ada_b, l1_ln_g, l1_ln_b, l1_ffa_w_in, l1_ffa_w_out, l1_ffb_w_in, l1_ffb_w_out, l1_w_uv, l1_b_uv, l1_sgu_ln_g, l1_sgu_ln_b, l1_w_s, l1_b_s, l1_w_out):
    batch, seq, d = x.shape
    assert d == D_MODEL and seq % SSD_CHUNK == 0
    tm = math.gcd(seq, 512)
    tq = math.gcd(seq, 256)
    x2 = x.reshape(batch * seq, d)

    mod, (ffa_in, ffa_out) = _ada(c, l0_ada_w, l0_ada_b, cast=(l0_ffa_w_in, l0_ffa_w_out))
    mod = mod.reshape(batch, N_MOD, d)
    x2, (ffb_in, ffb_out, mix_out) = _ffn(
        x2, mod, ffa_in, ffa_out, l0_ln_g, l0_ln_b, sub=0, seq=seq, tm=tm,
        cast=(l0_ffb_w_in, l0_ffb_w_out, l0_w_out))
    cos_t, sin_t = _rope_tables(seq)
    z, xs, bc, dt, q, k, v = _even_proj(
        x2, mod, _even_proj_weight(l0_w_in), l0_conv_w, l0_conv_b.reshape(1, -1),
        l0_q_norm_w.reshape(1, -1), _q_weight(l0_w_uq),
        l0_kv_norm_w.reshape(1, -1), _kv_weight(l0_w_ukv), cos_t, sin_t, seq=seq, tm=tm)
    y_ssd = _ssd(xs, bc, dt, z, _pad_lanes_row(l0_dt_bias),
                 _pad_lanes_row(l0_a_log), jnp.repeat(l0_d_skip, SSD_HEAD_DIM).reshape(1, -1),
                 l0_ssd_norm_w.reshape(1, -1), batch=batch, seq=seq,
                 seqs_per_step=math.gcd(batch, 4))
    y_att = _attention(q, k, v, batch=batch, seq=seq, tq=tq)
    x2 = _even_out(x2, mod, y_ssd, y_att, mix_out, l0_ln_g, l0_ln_b, seq=seq, tm=tm)
    x2, (ffa_in, ffa_out) = _ffn(x2, mod, ffb_in, ffb_out, l0_ln_g, l0_ln_b, sub=2, seq=seq, tm=tm,
                                 cast=(l1_ffa_w_in, l1_ffa_w_out))

    mod, _ = _ada(c, l1_ada_w, l1_ada_b)
    mod = mod.reshape(batch, N_MOD, d)
    x2, (w_uv, mix_out) = _ffn(x2, mod, ffa_in, ffa_out, l1_ln_g, l1_ln_b, sub=0, seq=seq, tm=tm,
                               cast=(l1_w_uv, l1_w_out))
    b_s_x = jnp.repeat(l1_b_s.T, SGU_GROUP_DIM, axis=1)
    x2, (ffb_in, ffb_out) = _sgu(
        x2, mod, w_uv, l1_b_uv.reshape(1, -1), l1_sgu_ln_g.reshape(1, -1),
        l1_sgu_ln_b.reshape(1, -1), l1_w_s, b_s_x, mix_out, l1_ln_g, l1_ln_b,
        seq=seq, tm=tm, cast=(l1_ffb_w_in, l1_ffb_w_out))
    x2, _ = _ffn(x2, mod, ffb_in, ffb_out, l1_ln_g, l1_ln_b, sub=2, seq=seq, tm=tm)
    return x2.reshape(batch, seq, d)
```
